```python
import math
import jax, jax.numpy as jnp
from jax import lax
import numpy as np

D_MODEL = 4096
BATCH = 4
SEQ = 2048
DEPTH = 4
DEC_BATCH = 8
DEC_SEQ = 8
PAST_LEN = 8192
PAGE_SIZE = 128

DA_HEADS = D_MODEL // 512
DA_HEAD_DIM = 128
DA_V_DIM = 2 * DA_HEAD_DIM
DA_QK_WIDTH = DA_HEADS * 2 * DA_HEAD_DIM
DA_WIDTH = DA_HEADS * DA_V_DIM
RET_HEADS = D_MODEL // 512
RET_QK_DIM = 256
RET_V_DIM = 256
RET_QK_WIDTH = RET_HEADS * RET_QK_DIM
RET_WIDTH = RET_HEADS * RET_V_DIM
RET_CHUNK = 128
Q_BLOCK = 128
D_FF = 4 * D_MODEL
N_MOD = 6
ROPE_BASE = 10000.0
EPS = 1e-6
IN_WIDTHS = (DA_QK_WIDTH, DA_QK_WIDTH, DA_WIDTH, RET_QK_WIDTH, RET_QK_WIDTH, RET_WIDTH, RET_WIDTH, D_MODEL, D_MODEL)
IN_WIDTH = sum(IN_WIDTHS)

kernel_name = 'hybrid_diffattn_retention_step'


def _rms_norm(x, gain=None):
    xf = x.astype(jnp.float32)
    y = xf * lax.rsqrt(jnp.mean(xf * xf, axis=-1, keepdims=True) + EPS)
    if gain is not None:
        y = y * gain.astype(jnp.float32)
    return y.astype(x.dtype)


def _rotary(x, pos):
    half = x.shape[-1] // 2
    inv = 1.0 / (ROPE_BASE ** jnp.linspace(0.0, 1.0, half, dtype=jnp.float32))
    ang = pos[:, None] * inv[None, :]
    cos = jnp.cos(ang)[None, :, None, :]
    sin = jnp.sin(ang)[None, :, None, :]
    x1, x2 = x[..., :half], x[..., half:]
    return jnp.concatenate([x1 * cos - x2 * sin, x2 * cos + x1 * sin], axis=-1)


def _log_gamma():
    return jnp.log(1.0 - 2.0 ** (-5.0 - jnp.arange(RET_HEADS, dtype=jnp.float32)))


def _retention(q, k, v, state0, chunk):
    b, length, h, _ = q.shape
    dv = v.shape[-1]
    n = length // chunk
    lg = _log_gamma()
    idx = jnp.arange(chunk, dtype=jnp.float32)
    rel = idx[:, None] - idx[None, :]
    causal = rel >= 0
    decay_mask = jnp.where(causal[None], jnp.exp(jnp.where(causal, rel, 0.0)[None] * lg[:, None, None]), 0.0)
    q_decay = jnp.exp((idx + 1.0)[:, None] * lg[None, :])
    k_decay = jnp.exp((chunk - 1.0 - idx)[:, None] * lg[None, :])
    c_decay = jnp.exp(chunk * lg)

    def to_chunks(t):
        return t.reshape(b, n, chunk, h, t.shape[-1]).swapaxes(0, 1)

    def step(s, inp):
        qc, kc, vc = inp
        scores = jnp.einsum('bihd,bjhd->bhij', qc, kc) * decay_mask[None]
        o = (jnp.einsum('bhij,bjhv->bihv', scores, vc)
             + jnp.einsum('bihd,bhdv->bihv', qc, s) * q_decay[None, :, :, None])
        s = s * c_decay[None, :, None, None] + jnp.einsum('bjhd,bjhv->bhdv', kc * k_decay[None, :, :, None], vc)
        return s, o

    s, o = lax.scan(step, state0, (to_chunks(q), to_chunks(k), to_chunks(v)))
    return o.swapaxes(0, 1).reshape(b, length, h, dv), s


def _diff_attend(q, k, v, q_pos, k_pos, lam):
    s = jnp.einsum('bqhcd,bkhcd->bhcqk', q.astype(jnp.float32), k.astype(jnp.float32)) * (DA_HEAD_DIM ** -0.5)
    mask = k_pos[None, :] <= q_pos[:, None]
    s = jnp.where(mask, s, -jnp.inf)
    p = jax.nn.softmax(s, axis=-1)
    a = p[:, :, 0] - lam * p[:, :, 1]
    return jnp.einsum('bhqk,bkhv->bqhv', a, v.astype(jnp.float32))


def setup_inputs(seed: int = 0) -> dict:
    key = jax.random.key(seed)
    ks = jax.random.split(key, 24)
    f32 = jnp.float32
    n_pages = PAST_LEN // PAGE_SIZE
    n_used = DEC_BATCH * n_pages
    n_pool = n_used + max(1, n_used // 4)

    def nrm(k, shape, scale=1.0):
        return jax.random.normal(k, shape, f32) * scale

    page_table = jax.random.permutation(ks[5], n_pool)[:n_used].reshape(DEC_BATCH, n_pages).astype(jnp.int32)
    return {
        'x_prompt': nrm(ks[0], (BATCH, SEQ, D_MODEL)),
        'x_sample': nrm(ks[1], (DEC_BATCH, DEC_SEQ, D_MODEL)),
        'cache_k': nrm(ks[2], (DEPTH, n_pool, PAGE_SIZE, DA_HEADS, 2 * DA_HEAD_DIM)),
        'cache_v': nrm(ks[3], (DEPTH, n_pool, PAGE_SIZE, DA_HEADS, DA_V_DIM)),
        'state_ret': nrm(ks[4], (DEPTH, DEC_BATCH, RET_HEADS, RET_QK_DIM, RET_V_DIM)),
        'page_table': page_table,
        'c_prompt': nrm(ks[6], (BATCH, D_MODEL)),
        'c_sample': nrm(ks[7], (DEC_BATCH, D_MODEL)),
        'w_ada': nrm(ks[8], (DEPTH, D_MODEL, N_MOD * D_MODEL), 0.5 * D_MODEL ** -0.5),
        'b_ada': nrm(ks[9], (DEPTH, N_MOD * D_MODEL), 0.02),
        'norm1': 1.0 + nrm(ks[10], (DEPTH, D_MODEL), 0.02),
        'norm2': 1.0 + nrm(ks[11], (DEPTH, D_MODEL), 0.02),
        'w_in': nrm(ks[12], (DEPTH, D_MODEL, IN_WIDTH), D_MODEL ** -0.5),
        'lambda_q1': nrm(ks[13], (DEPTH, DA_HEAD_DIM), 0.1),
        'lambda_k1': nrm(ks[14], (DEPTH, DA_HEAD_DIM), 0.1),
        'lambda_q2': nrm(ks[15], (DEPTH, DA_HEAD_DIM), 0.1),
        'lambda_k2': nrm(ks[16], (DEPTH, DA_HEAD_DIM), 0.1),
        'da_subln': 1.0 + nrm(ks[17], (DEPTH, DA_V_DIM), 0.02),
        'w_pa': nrm(ks[18], (DEPTH, DA_WIDTH, D_MODEL), DA_WIDTH ** -0.5),
        'w_pr': nrm(ks[19], (DEPTH, RET_WIDTH, D_MODEL), RET_WIDTH ** -0.5),
        'w_o': nrm(ks[20], (DEPTH, D_MODEL, D_MODEL), D_MODEL ** -0.5),
        'w_up': nrm(ks[21], (DEPTH, D_MODEL, D_FF), D_MODEL ** -0.5),
        'w_down': nrm(ks[22], (DEPTH, D_FF, D_MODEL), D_FF ** -0.5),
        'norm_f': 1.0 + nrm(ks[23], (D_MODEL,), 0.02),
    }


def reference(x_prompt, x_sample, cache_k, cache_v, state_ret, page_table, c_prompt, c_sample,
              w_ada, b_ada, norm1, norm2, w_in, lambda_q1, lambda_k1, lambda_q2, lambda_k2,
              da_subln, w_pa, w_pr, w_o, w_up, w_down, norm_f):
    f32 = jnp.float32
    split_points = np.cumsum(IN_WIDTHS)[:-1].tolist()

    def run_layer(x, c, l, past_k, past_v, state0):
        bn, length, _ = x.shape
        dt = x.dtype
        past_len = 0 if past_k is None else past_k.shape[1]
        mod = (jax.nn.silu(c) @ w_ada[l] + b_ada[l]).astype(dt).reshape(bn, N_MOD, 1, D_MODEL)
        h = _rms_norm(x, norm1[l]) * (1 + mod[:, 1]) + mod[:, 0]
        proj = h @ w_in[l]
        qa, ka, va, qr, kr, vr, gr, gate_a, gate_r = jnp.split(proj, split_points, axis=-1)
        qa = qa.reshape(bn, length, DA_HEADS, 2, DA_HEAD_DIM)
        ka = ka.reshape(bn, length, DA_HEADS, 2, DA_HEAD_DIM)
        va = va.reshape(bn, length, DA_HEADS, DA_V_DIM)
        q_pos = past_len + jnp.arange(length, dtype=jnp.int32)

        lam_init = 0.8 - 0.6 * math.exp(-0.3 * l)
        lam = (jnp.exp(jnp.sum(lambda_q1[l].astype(f32) * lambda_k1[l].astype(f32)))
               - jnp.exp(jnp.sum(lambda_q2[l].astype(f32) * lambda_k2[l].astype(f32))) + lam_init)
        if past_k is None:
            qb = min(Q_BLOCK, length)

            def blk(i):
                q_blk = lax.dynamic_slice_in_dim(qa, i * qb, qb, axis=1)
                p_blk = lax.dynamic_slice_in_dim(q_pos, i * qb, qb)
                return _diff_attend(q_blk, ka, va, p_blk, q_pos, lam)

            o_a = lax.map(blk, jnp.arange(length // qb))
            o_a = jnp.moveaxis(o_a, 0, 1).reshape(bn, length, DA_HEADS, DA_V_DIM)
        else:
            k_all = jnp.concatenate([past_k.astype(dt), ka], axis=1)
            v_all = jnp.concatenate([past_v.astype(dt), va], axis=1)
            k_pos = jnp.arange(past_len + length, dtype=jnp.int32)
            o_a = _diff_attend(qa, k_all, v_all, q_pos, k_pos, lam)
        o_a = (_rms_norm(o_a, da_subln[l]) * (1.0 - lam_init)).astype(dt).reshape(bn, length, DA_WIDTH)

        pos_f = q_pos.astype(f32)
        qr = _rotary(qr.reshape(bn, length, RET_HEADS, RET_QK_DIM).astype(f32), pos_f)
        kr = _rotary(kr.reshape(bn, length, RET_HEADS, RET_QK_DIM).astype(f32), pos_f) * (RET_QK_DIM ** -0.5)
        vr = vr.reshape(bn, length, RET_HEADS, RET_V_DIM).astype(f32)
        chunk = length if length <= RET_CHUNK else RET_CHUNK
        o_r, s_new = _retention(qr, kr, vr, state0.astype(f32), chunk)
        o_r = _rms_norm(o_r).reshape(bn, length, RET_WIDTH).astype(dt) * jax.nn.silu(gr)

        mixed = jax.nn.sigmoid(gate_a) * (o_a @ w_pa[l]) + jax.nn.sigmoid(gate_r) * (o_r @ w_pr[l])
        x = x + mod[:, 2] * (mixed @ w_o[l])

        h2 = _rms_norm(x, norm2[l]) * (1 + mod[:, 4]) + mod[:, 3]
        x = x + mod[:, 5] * (jnp.square(jax.nn.relu(h2 @ w_up[l])) @ w_down[l])
        return x, ka.reshape(bn, length, DA_HEADS, 2 * DA_HEAD_DIM), va, s_new.astype(dt)

    xp, xs = x_prompt, x_sample
    bp, bs = x_prompt.shape[0], x_sample.shape[0]
    zero_state = jnp.zeros((bp, RET_HEADS, RET_QK_DIM, RET_V_DIM), x_prompt.dtype)
    kp, vp, sp, ksm, vsm, ssm = [], [], [], [], [], []
    for l in range(DEPTH):
        xp, k_new, v_new, s_new = run_layer(xp, c_prompt, l, None, None, zero_state)
        kp.append(k_new); vp.append(v_new); sp.append(s_new)
        past_k = cache_k[l, page_table].reshape(bs, -1, DA_HEADS, 2, DA_HEAD_DIM)
        past_v = cache_v[l, page_table].reshape(bs, -1, DA_HEADS, DA_V_DIM)
        xs, k_new, v_new, s_new = run_layer(xs, c_sample, l, past_k, past_v, state_ret[l])
        ksm.append(k_new); vsm.append(v_new); ssm.append(s_new)
    y_prompt = _rms_norm(xp, norm_f)
    y_sample = _rms_norm(xs, norm_f)
    return (y_prompt, y_sample, jnp.stack(kp), jnp.stack(vp), jnp.stack(sp), jnp.stack(ksm), jnp.stack(vsm), jnp.stack(ssm))
```

```python
import functools
import math

import jax
import jax.numpy as jnp
from jax import lax
from jax.experimental import pallas as pl
from jax.experimental.pallas import tpu as pltpu

F32 = jnp.float32
BF16 = jnp.bfloat16

EPS = 1e-6
ROPE_BASE = 10000.0
RET_CHUNK = 128
N_MOD = 6
HEAD_W = 256
VMEM_LIMIT_BYTES = 60 * 1024 * 1024

NT_DIMS = (((1,), (1,)), ((), ()))
TN_DIMS = (((0,), (0,)), ((), ()))


def _params(*sem):
    return pltpu.CompilerParams(dimension_semantics=sem, vmem_limit_bytes=VMEM_LIMIT_BYTES)


def _sigmoid(x):
    return 1.0 / (1.0 + jnp.exp(-x))


def _tile(n, want):
    t = min(n, want)
    while n % t:
        t -= 1
    return t


def _ada_kernel(c_ref, w_ref, b_ref, o_ref):
    c = c_ref[...]
    a = (c * _sigmoid(c)).astype(BF16)
    o_ref[...] = jnp.dot(a, w_ref[...].astype(BF16), preferred_element_type=F32) + b_ref[...]


def _ada(c_all, w_ada, b_ada):
    depth, d, n = w_ada.shape
    rows = c_all.shape[0]
    tn = _tile(n, 512)
    return pl.pallas_call(
        _ada_kernel,
        grid=(depth, n // tn),
        in_specs=[
            pl.BlockSpec((rows, d), lambda l, j: (0, 0)),
            pl.BlockSpec((None, d, tn), lambda l, j: (l, 0, j)),
            pl.BlockSpec((None, 1, tn), lambda l, j: (l, 0, j)),
        ],
        out_specs=pl.BlockSpec((None, rows, tn), lambda l, j: (l, 0, j)),
        out_shape=jax.ShapeDtypeStruct((depth, rows, n), F32),
        compiler_params=_params("parallel", "parallel"),
        name="ada",
    )(c_all, w_ada, b_ada.reshape(depth, 1, n))


def _norm_mod_kernel(x_ref, g_ref, sc_ref, sh_ref, o_ref):
    x = x_ref[...]
    y = x * lax.rsqrt(jnp.mean(x * x, axis=-1, keepdims=True) + EPS) * g_ref[...]
    o_ref[...] = (y * (1.0 + sc_ref[...]) + sh_ref[...]).astype(o_ref.dtype)


def _norm_mod(x, gain, mods, l, m_scale, m_shift):
    m, d = x.shape
    groups, r = mods.shape[2], mods.shape[3]
    tm = m // groups if r > 1 else _tile(m // groups, 256)
    per_group = (m // groups) // tm
    return pl.pallas_call(
        _norm_mod_kernel,
        grid=(m // tm,),
        in_specs=[
            pl.BlockSpec((tm, d), lambda i: (i, 0)),
            pl.BlockSpec((None, 1, d), lambda i: (l, 0, 0)),
            pl.BlockSpec((None, None, None, r, d), lambda i: (l, m_scale, i // per_group, 0, 0)),
            pl.BlockSpec((None, None, None, r, d), lambda i: (l, m_shift, i // per_group, 0, 0)),
        ],
        out_specs=pl.BlockSpec((tm, d), lambda i: (i, 0)),
        out_shape=jax.ShapeDtypeStruct((m, d), BF16),
        compiler_params=_params("parallel"),
        name="norm_mod",
    )(x, gain.reshape(gain.shape[0], 1, d), mods, mods)


def _norm_kernel(x_ref, g_ref, o_ref):
    x = x_ref[...]
    o_ref[...] = x * lax.rsqrt(jnp.mean(x * x, axis=-1, keepdims=True) + EPS) * g_ref[...]


def _final_norm(x, gain):
    m, d = x.shape
    tm = _tile(m, 256)
    return pl.pallas_call(
        _norm_kernel,
        grid=(m // tm,),
        in_specs=[pl.BlockSpec((tm, d), lambda i: (i, 0)), pl.BlockSpec((1, d), lambda i: (0, 0))],
        out_specs=pl.BlockSpec((tm, d), lambda i: (i, 0)),
        out_shape=jax.ShapeDtypeStruct((m, d), F32),
        compiler_params=_params("parallel"),
        name="final_norm",
    )(x, gain.reshape(1, d))


def _mm_kernel(*refs, nk, epilogue):
    if epilogue == "resid":
        a_ref, w_ref, x_ref, g_ref, o_ref = refs[:5]
        scratch = refs[5:]
    else:
        a_ref, w_ref, o_ref = refs[:3]
        scratch = refs[3:]
    part = jnp.dot(a_ref[...], w_ref[...].astype(BF16), preferred_element_type=F32)

    def finish(acc):
        if epilogue == "relu2":
            r = jnp.maximum(acc, 0.0)
            o_ref[...] = (r * r).astype(o_ref.dtype)
        elif epilogue == "resid":
            o_ref[...] = x_ref[...] + g_ref[...] * acc
        else:
            o_ref[...] = acc.astype(o_ref.dtype)

    if nk == 1:
        finish(part)
        return
    acc_ref, = scratch
    k = pl.program_id(2)

    @pl.when(k == 0)
    def _():
        acc_ref[...] = part

    @pl.when(jnp.logical_and(k > 0, k < nk - 1))
    def _():
        acc_ref[...] += part

    @pl.when(k == nk - 1)
    def _():
        finish(acc_ref[...] + part)


def _matmul(a, w, l, n, out_dtype, epilogue="none", resid=None, mods=None, m_gate=None,
            tm_want=1024, tn_want=512, tk_want=4096):
    m, kdim = a.shape
    groups = mods.shape[2] if epilogue == "resid" else 1
    tm = _tile(m // groups, tm_want)
    tn = _tile(n, tn_want)
    tk = _tile(kdim, tk_want)
    nk = kdim // tk
    in_specs = [
        pl.BlockSpec((tm, tk), lambda i, j, k: (i, k)),
        pl.BlockSpec((None, tk, tn), lambda i, j, k: (l, k, j)),
    ]
    args = [a, w]
    if epilogue == "resid":
        r = mods.shape[3]
        assert r == 1 or (groups == 1 and r == tm == m)
        per_group = (m // groups) // tm
        in_specs += [
            pl.BlockSpec((tm, tn), lambda i, j, k: (i, j)),
            pl.BlockSpec((None, None, None, r, tn), lambda i, j, k: (l, m_gate, i // per_group, 0, j)),
        ]
        args += [resid, mods]
    return pl.pallas_call(
        functools.partial(_mm_kernel, nk=nk, epilogue=epilogue),
        grid=(m // tm, n // tn, nk),
        in_specs=in_specs,
        out_specs=pl.BlockSpec((tm, tn), lambda i, j, k: (i, j)),
        out_shape=jax.ShapeDtypeStruct((m, n), out_dtype),
        scratch_shapes=[pltpu.VMEM((tm, tn), F32)] if nk > 1 else [],
        compiler_params=_params("parallel", "parallel", "arbitrary"),
        name="mm_" + epilogue,
    )(*args)


def _mix_kernel(oa_ref, or_ref, wpa_ref, wpr_ref, ga_ref, gr_ref, o_ref):
    pa = jnp.dot(oa_ref[...], wpa_ref[...].astype(BF16), preferred_element_type=F32)
    pr = jnp.dot(or_ref[...], wpr_ref[...].astype(BF16), preferred_element_type=F32)
    o_ref[...] = (_sigmoid(ga_ref[...]) * pa + _sigmoid(gr_ref[...]) * pr).astype(o_ref.dtype)


def _mix(o_a, o_r, w_pa, w_pr, proj, l, col_ga, col_gr, d):
    m, ka = o_a.shape
    kr = o_r.shape[1]
    tm = _tile(m, 1024)
    tn = _tile(d, 512)
    return pl.pallas_call(
        _mix_kernel,
        grid=(m // tm, d // tn),
        in_specs=[
            pl.BlockSpec((tm, ka), lambda i, j: (i, 0)),
            pl.BlockSpec((tm, kr), lambda i, j: (i, 0)),
            pl.BlockSpec((None, ka, tn), lambda i, j: (l, 0, j)),
            pl.BlockSpec((None, kr, tn), lambda i, j: (l, 0, j)),
            pl.BlockSpec((tm, tn), lambda i, j: (i, col_ga // tn + j)),
            pl.BlockSpec((tm, tn), lambda i, j: (i, col_gr // tn + j)),
        ],
        out_specs=pl.BlockSpec((tm, tn), lambda i, j: (i, j)),
        out_shape=jax.ShapeDtypeStruct((m, d), BF16),
        compiler_params=_params("parallel", "parallel"),
        name="mix",
    )(o_a, o_r, w_pa, w_pr, proj, proj)


def _lambda(lq1, lk1, lq2, lk2, lam_init):
    return (jnp.exp(jnp.sum(lq1[...] * lk1[...], axis=-1, keepdims=True))
            - jnp.exp(jnp.sum(lq2[...] * lk2[...], axis=-1, keepdims=True)) + lam_init)


def _subln(o, gain, lam_init):
    y = o * lax.rsqrt(jnp.mean(o * o, axis=-1, keepdims=True) + EPS) * gain
    return y * (1.0 - lam_init)


def _attn_prompt_kernel(q_ref, k_ref, v_ref, lq1, lk1, lq2, lk2, g_ref, o_ref, kb, vb, s_ref, *, lam_init, t, scale):
    qi = pl.program_id(2)
    half = HEAD_W // 2

    @pl.when(qi == 0)
    def _():
        kb[...] = k_ref[...].astype(BF16)
        vb[...] = v_ref[...].astype(BF16)

    lam = _lambda(lq1, lk1, lq2, lk2, lam_init)
    q = q_ref[...].astype(BF16)
    q1, q2 = q[:, :half], q[:, half:]

    def scores(j):
        kj = kb[pl.ds(pl.multiple_of(j * t, t), t), :]
        s1 = lax.dot_general(q1, kj[:, :half], NT_DIMS, preferred_element_type=F32) * scale
        s2 = lax.dot_general(q2, kj[:, half:], NT_DIMS, preferred_element_type=F32) * scale
        return s1, s2

    def score_body(j, carry):
        m1, m2 = carry
        s1, s2 = scores(j)
        s_ref[0, j] = s1
        s_ref[1, j] = s2
        return (jnp.maximum(m1, jnp.max(s1, axis=-1, keepdims=True)),
                jnp.maximum(m2, jnp.max(s2, axis=-1, keepdims=True)))

    neg = jnp.full((t, 1), -jnp.inf, F32)
    m1, m2 = lax.fori_loop(0, qi, score_body, (neg, neg))
    s1, s2 = scores(qi)
    keep = lax.broadcasted_iota(jnp.int32, (t, t), 1) <= lax.broadcasted_iota(jnp.int32, (t, t), 0)
    s1 = jnp.where(keep, s1, -jnp.inf)
    s2 = jnp.where(keep, s2, -jnp.inf)
    s_ref[0, qi] = s1
    s_ref[1, qi] = s2
    m1 = jnp.maximum(m1, jnp.max(s1, axis=-1, keepdims=True))
    m2 = jnp.maximum(m2, jnp.max(s2, axis=-1, keepdims=True))

    def exp_body(j, carry):
        l1, l2 = carry
        e1 = jnp.exp(s_ref[0, j] - m1)
        e2 = jnp.exp(s_ref[1, j] - m2)
        s_ref[0, j] = e1
        s_ref[1, j] = e2
        return l1 + jnp.sum(e1, axis=-1, keepdims=True), l2 + jnp.sum(e2, axis=-1, keepdims=True)

    zero = jnp.zeros((t, 1), F32)
    l1, l2 = lax.fori_loop(0, qi + 1, exp_body, (zero, zero))
    r1 = 1.0 / l1
    r2 = lam / l2

    def pv_body(j, acc):
        a = (s_ref[0, j] * r1 - s_ref[1, j] * r2).astype(BF16)
        vj = vb[pl.ds(pl.multiple_of(j * t, t), t), :]
        return acc + jnp.dot(a, vj, preferred_element_type=F32)

    o = lax.fori_loop(0, qi + 1, pv_body, jnp.zeros((t, HEAD_W), F32))
    o_ref[...] = _subln(o, g_ref[...], lam_init).astype(o_ref.dtype)


def _attn_prompt(proj, lams, subln, l, lam_init, bsz, seq, heads):
    m = proj.shape[0]
    t = _tile(seq, 256)
    nq = seq // t
    lam_spec = pl.BlockSpec((None, 1, HEAD_W // 2), lambda b, h, i: (l, 0, 0))
    return pl.pallas_call(
        functools.partial(_attn_prompt_kernel, lam_init=lam_init, t=t, scale=(HEAD_W // 2) ** -0.5),
        grid=(bsz, heads, nq),
        in_specs=[
            pl.BlockSpec((t, HEAD_W), lambda b, h, i: (b * nq + i, h)),
            pl.BlockSpec((seq, HEAD_W), lambda b, h, i: (b, heads + h)),
            pl.BlockSpec((seq, HEAD_W), lambda b, h, i: (b, 2 * heads + h)),
            lam_spec, lam_spec, lam_spec, lam_spec,
            pl.BlockSpec((None, 1, HEAD_W), lambda b, h, i: (l, 0, 0)),
        ],
        out_specs=pl.BlockSpec((t, HEAD_W), lambda b, h, i: (b * nq + i, h)),
        out_shape=jax.ShapeDtypeStruct((m, heads * HEAD_W), BF16),
        scratch_shapes=[
            pltpu.VMEM((seq, HEAD_W), BF16),
            pltpu.VMEM((seq, HEAD_W), BF16),
            pltpu.VMEM((2, nq, t, t), F32),
        ],
        compiler_params=_params("parallel", "parallel", "arbitrary"),
        name="attn_prompt",
    )(proj, proj, proj, *lams, subln)


def _attn_decode_kernel(pt_ref, q_ref, kn_ref, vn_ref, kp_ref, vp_ref, lq1, lk1, lq2, lk2, g_ref, o_ref,
                        qbd, m_ref, l_ref, acc_ref, *, lam_init, heads, n_new, page, scale):
    p = pl.program_id(1)
    half = HEAD_W // 2
    rows = 2 * n_new

    @pl.when(p == 0)
    def _():
        q = q_ref[...]
        z = jnp.zeros((n_new, half), F32)
        for h in range(heads):
            q1 = q[:, h * HEAD_W:h * HEAD_W + half]
            q2 = q[:, h * HEAD_W + half:(h + 1) * HEAD_W]
            top = jnp.concatenate([q1, z], axis=1)
            bot = jnp.concatenate([z, q2], axis=1)
            qbd[h] = jnp.concatenate([top, bot], axis=0).astype(BF16)
        m_ref[...] = jnp.full(m_ref.shape, -jnp.inf, F32)
        l_ref[...] = jnp.zeros(l_ref.shape, F32)
        acc_ref[...] = jnp.zeros(acc_ref.shape, F32)

    def absorb(k_page, v_page, keep):
        for h in range(heads):
            kh = k_page(h).astype(BF16)
            vh = v_page(h).astype(BF16)
            s = lax.dot_general(qbd[h], kh, NT_DIMS, preferred_element_type=F32) * scale
            if keep is not None:
                s = jnp.where(keep, s, -jnp.inf)
            m_old = m_ref[h]
            m_new = jnp.maximum(m_old, jnp.max(s, axis=-1, keepdims=True))
            alpha = jnp.exp(m_old - m_new)
            e = jnp.exp(s - m_new)
            l_ref[h] = alpha * l_ref[h] + jnp.sum(e, axis=-1, keepdims=True)
            acc_ref[h] = alpha * acc_ref[h] + jnp.dot(e.astype(BF16), vh, preferred_element_type=F32)
            m_ref[h] = m_new

    def head_cols(ref):
        return lambda h: ref[:, h * HEAD_W:(h + 1) * HEAD_W]

    absorb(lambda h: kp_ref[:, h, :], lambda h: vp_ref[:, h, :], None)

    @pl.when(p == pl.num_programs(1) - 1)
    def _():
        pad = jnp.zeros((page - n_new, HEAD_W), F32)
        key = lax.broadcasted_iota(jnp.int32, (rows, page), 1)
        tok = lax.broadcasted_iota(jnp.int32, (rows, page), 0) % n_new
        absorb(lambda h: jnp.concatenate([head_cols(kn_ref)(h), pad], axis=0),
               lambda h: jnp.concatenate([head_cols(vn_ref)(h), pad], axis=0), key <= tok)
        lam = _lambda(lq1, lk1, lq2, lk2, lam_init)
        outs = []
        for h in range(heads):
            acc = acc_ref[h]
            lsum = l_ref[h]
            o = acc[:n_new] / lsum[:n_new] - lam * (acc[n_new:] / lsum[n_new:])
            outs.append(_subln(o, g_ref[...], lam_init))
        o_ref[...] = jnp.concatenate(outs, axis=1).astype(o_ref.dtype)


def _attn_decode(proj, cache_k, cache_v, page_table, lams, subln, l, lam_init, bsz, n_new, heads):
    depth, n_pool, page = cache_k.shape[:3]
    width = heads * HEAD_W
    n_pages = page_table.shape[1]
    lam_spec = pl.BlockSpec((None, 1, HEAD_W // 2), lambda b, p, pt: (l, 0, 0))
    page_spec = pl.BlockSpec((None, None, page, heads, HEAD_W), lambda b, p, pt: (l, pt[b, p], 0, 0, 0))
    grid_spec = pltpu.PrefetchScalarGridSpec(
        num_scalar_prefetch=1,
        grid=(bsz, n_pages),
        in_specs=[
            pl.BlockSpec((n_new, width), lambda b, p, pt: (b, 0)),
            pl.BlockSpec((n_new, width), lambda b, p, pt: (b, 1)),
            pl.BlockSpec((n_new, width), lambda b, p, pt: (b, 2)),
            page_spec, page_spec,
            lam_spec, lam_spec, lam_spec, lam_spec,
            pl.BlockSpec((None, 1, HEAD_W), lambda b, p, pt: (l, 0, 0)),
        ],
        out_specs=pl.BlockSpec((None, n_new, width), lambda b, p, pt: (b, 0, 0)),
        scratch_shapes=[
            pltpu.VMEM((heads, 2 * n_new, HEAD_W), BF16),
            pltpu.VMEM((heads, 2 * n_new, 1), F32),
            pltpu.VMEM((heads, 2 * n_new, 1), F32),
            pltpu.VMEM((heads, 2 * n_new, HEAD_W), F32),
        ],
    )
    out = pl.pallas_call(
        functools.partial(_attn_decode_kernel, lam_init=lam_init, heads=heads, n_new=n_new, page=page,
                          scale=(HEAD_W // 2) ** -0.5),
        grid_spec=grid_spec,
        out_shape=jax.ShapeDtypeStruct((bsz, n_new, width), BF16),
        compiler_params=_params("parallel", "arbitrary"),
        name="attn_decode",
    )(page_table, proj, proj, proj, cache_k, cache_v, *lams, subln)
    return out.reshape(bsz * n_new, width)


def _retention_kernel(*refs, nb, seq, chunk, has_state, mxu_dtype):
    if has_state:
        q_ref, k_ref, v_ref, g_ref, cos_ref, sin_ref, dm_ref, qd_ref, kd_ref, cd_ref, s0_ref, o_ref, s_ref = refs
    else:
        q_ref, k_ref, v_ref, g_ref, cos_ref, sin_ref, dm_ref, qd_ref, kd_ref, cd_ref, o_ref, s_ref = refs
    half = HEAD_W // 2
    n_chunks = seq // chunk
    k_scale = HEAD_W ** -0.5
    dmask = dm_ref[...]
    qdec = qd_ref[...]
    kdec = kd_ref[...]
    cdec = cd_ref[...]

    def rotate(x, cos, sin):
        x1, x2 = x[:, :half], x[:, half:]
        return jnp.concatenate([x1 * cos - x2 * sin, x2 * cos + x1 * sin], axis=1)

    def chunk_out(rows, pos, state):
        cos = cos_ref[pos, :]
        sin = sin_ref[pos, :]
        qc = rotate(q_ref[rows, :], cos, sin).astype(mxu_dtype)
        kf = rotate(k_ref[rows, :], cos, sin) * k_scale
        vc = v_ref[rows, :].astype(mxu_dtype)
        scores = lax.dot_general(qc, kf.astype(mxu_dtype), NT_DIMS, preferred_element_type=F32) * dmask
        o = (jnp.dot(scores.astype(mxu_dtype), vc, preferred_element_type=F32)
             + jnp.dot(qc, state.astype(mxu_dtype), preferred_element_type=F32) * qdec)
        new_state = state * cdec + lax.dot_general((kf * kdec).astype(mxu_dtype), vc, TN_DIMS,
                                                   preferred_element_type=F32)
        o = o * lax.rsqrt(jnp.mean(o * o, axis=-1, keepdims=True) + EPS)
        g = g_ref[rows, :]
        return o * (g * _sigmoid(g)), new_state

    for b in range(nb):
        state0 = s0_ref[b] if has_state else jnp.zeros((HEAD_W, HEAD_W), F32)
        if n_chunks == 1:
            out, state = chunk_out(pl.ds(b * seq, chunk), pl.ds(0, chunk), state0)
            o_ref[pl.ds(b * seq, chunk), :] = out.astype(o_ref.dtype)
        else:
            def body(c, state, b=b):
                start = pl.multiple_of(c * chunk, chunk)
                out, state = chunk_out(pl.ds(b * seq + start, chunk), pl.ds(start, chunk), state)
                o_ref[pl.ds(b * seq + start, chunk), :] = out.astype(o_ref.dtype)
                return state
            state = lax.fori_loop(0, n_chunks, body, state0)
        s_ref[b] = state


def _retention(proj, state0, l, tables, bsz, seq, heads, col_q, nb, out_dtype, mxu_dtype):
    cos, sin, dmask, qdec, kdec, cdec = tables
    m = proj.shape[0]
    chunk = dmask.shape[1]
    cb = col_q // HEAD_W
    rows = nb * seq

    def col(which):
        return pl.BlockSpec((rows, HEAD_W), lambda g, h: (g, cb + which * heads + h))

    in_specs = [
        col(0), col(1), col(2), col(3),
        pl.BlockSpec((seq, HEAD_W // 2), lambda g, h: (0, 0)),
        pl.BlockSpec((seq, HEAD_W // 2), lambda g, h: (0, 0)),
        pl.BlockSpec((None, chunk, chunk), lambda g, h: (h, 0, 0)),
        pl.BlockSpec((None, chunk, 1), lambda g, h: (h, 0, 0)),
        pl.BlockSpec((None, chunk, 1), lambda g, h: (h, 0, 0)),
        pl.BlockSpec((None, 1, 1), lambda g, h: (h, 0, 0)),
    ]
    args = [proj, proj, proj, proj, cos, sin, dmask, qdec, kdec, cdec]
    has_state = state0 is not None
    if has_state:
        in_specs.append(pl.BlockSpec((None, nb, None, HEAD_W, HEAD_W), lambda g, h: (l, g, h, 0, 0)))
        args.append(state0)
    return pl.pallas_call(
        functools.partial(_retention_kernel, nb=nb, seq=seq, chunk=chunk, has_state=has_state, mxu_dtype=mxu_dtype),
        grid=(bsz // nb, heads),
        in_specs=in_specs,
        out_specs=[
            pl.BlockSpec((rows, HEAD_W), lambda g, h: (g, h)),
            pl.BlockSpec((nb, None, HEAD_W, HEAD_W), lambda g, h: (g, h, 0, 0)),
        ],
        out_shape=[
            jax.ShapeDtypeStruct((m, heads * HEAD_W), out_dtype),
            jax.ShapeDtypeStruct((bsz, heads, HEAD_W, HEAD_W), F32),
        ],
        compiler_params=_params("parallel", "parallel"),
        name="retention",
    )(*args)


def _retention_tables(seq, first_pos, heads):
    half = HEAD_W // 2
    inv = 1.0 / (ROPE_BASE ** jnp.linspace(0.0, 1.0, half, dtype=F32))
    pos = (first_pos + jnp.arange(seq, dtype=jnp.int32)).astype(F32)
    ang = pos[:, None] * inv[None, :]
    chunk = seq if seq <= RET_CHUNK else RET_CHUNK
    lg = jnp.log(1.0 - 2.0 ** (-5.0 - jnp.arange(heads, dtype=F32)))
    idx = jnp.arange(chunk, dtype=F32)
    rel = idx[:, None] - idx[None, :]
    causal = rel >= 0
    dmask = jnp.where(causal[None], jnp.exp(jnp.where(causal, rel, 0.0)[None] * lg[:, None, None]), 0.0)
    qdec = jnp.exp((idx + 1.0)[None, :] * lg[:, None])[:, :, None]
    kdec = jnp.exp((chunk - 1.0 - idx)[None, :] * lg[:, None])[:, :, None]
    cdec = jnp.exp(chunk * lg)[:, None, None]
    return jnp.cos(ang), jnp.sin(ang), dmask, qdec, kdec, cdec


def kernel(x_prompt, x_sample, cache_k, cache_v, state_ret, page_table, c_prompt, c_sample, w_ada, b_ada, norm1, norm2, w_in, lambda_q1, lambda_k1, lambda_q2, lambda_k2, da_subln, w_pa, w_pr, w_o, w_up, w_down, norm_f):
    bp, seq, d = x_prompt.shape
    bs, n_new, _ = x_sample.shape
    depth = w_in.shape[0]
    page, heads = cache_k.shape[2], cache_k.shape[3]
    r_heads = state_ret.shape[2]
    assert cache_k.shape[4] == cache_v.shape[4] == state_ret.shape[3] == state_ret.shape[4] == HEAD_W
    assert heads == r_heads
    past_len = page_table.shape[1] * page
    d_ff = w_up.shape[2]
    hw = heads * HEAD_W
    col_k, col_v, col_qr = hw, 2 * hw, 3 * hw
    col_ga = 7 * hw
    col_gr = col_ga + d

    c_rows = -(-(bp + bs) // 8) * 8
    c_all = jnp.concatenate([c_prompt, c_sample, jnp.zeros((c_rows - bp - bs, d), F32)], axis=0)
    mod = _ada(c_all, w_ada, b_ada)
    mod_p = mod[:, :bp].reshape(depth, bp, N_MOD, 1, d).transpose(0, 2, 1, 3, 4)
    mod_s = mod[:, bp:bp + bs].reshape(depth, bs, N_MOD, d).transpose(0, 2, 1, 3)
    mod_s = jnp.repeat(mod_s, n_new, axis=2)[:, :, None]

    lams = [a.reshape(depth, 1, -1) for a in (lambda_q1, lambda_k1, lambda_q2, lambda_k2)]
    subln = da_subln.reshape(depth, 1, HEAD_W)
    tab_p = _retention_tables(seq, 0, heads)
    tab_s = _retention_tables(n_new, past_len, heads)

    def dense_tail(x, mods, l, proj, o_a, o_r):
        mixed = _mix(o_a, o_r, w_pa, w_pr, proj, l, col_ga, col_gr, d)
        x = _matmul(mixed, w_o, l, d, F32, "resid", x, mods, 2)
        h2 = _norm_mod(x, norm2, mods, l, 4, 3)
        u = _matmul(h2, w_up, l, d_ff, BF16, "relu2")
        return _matmul(u, w_down, l, d, F32, "resid", x, mods, 5)

    xp = x_prompt.reshape(bp * seq, d)
    xs = x_sample.reshape(bs * n_new, d)
    kp, vp, sp, ksm, vsm, ssm = [], [], [], [], [], []
    for l in range(depth):
        lam_init = 0.8 - 0.6 * math.exp(-0.3 * l)

        h = _norm_mod(xp, norm1, mod_p, l, 1, 0)
        proj = _matmul(h, w_in, l, w_in.shape[2], F32)
        o_a = _attn_prompt(proj, lams, subln, l, lam_init, bp, seq, heads)
        o_r, s_new = _retention(proj, None, l, tab_p, bp, seq, heads, col_qr, 1, BF16, BF16)
        kp.append(proj[:, col_k:col_k + hw].reshape(bp, seq, heads, HEAD_W))
        vp.append(proj[:, col_v:col_v + hw].reshape(bp, seq, heads, HEAD_W))
        sp.append(s_new)
        xp = dense_tail(xp, mod_p, l, proj, o_a, o_r)

        h = _norm_mod(xs, norm1, mod_s, l, 1, 0)
        proj = _matmul(h, w_in, l, w_in.shape[2], F32)
        o_a = _attn_decode(proj, cache_k, cache_v, page_table, lams, subln, l, lam_init, bs, n_new, heads)
        o_r, s_new = _retention(proj, state_ret, l, tab_s, bs, n_new, heads, col_qr, bs, F32, F32)
        ksm.append(proj[:, col_k:col_k + hw].reshape(bs, n_new, heads, HEAD_W))
        vsm.append(proj[:, col_v:col_v + hw].reshape(bs, n_new, heads, HEAD_W))
        ssm.append(s_new)
        xs = dense_tail(xs, mod_s, l, proj, o_a, o_r.astype(BF16))

    y_prompt = _final_norm(xp, norm_f).reshape(bp, seq, d)
    y_sample = _final_norm(xs, norm_f).reshape(bs, n_new, d)
    return (y_prompt, y_sample, jnp.stack(kp), jnp.stack(vp), jnp.stack(sp),
            jnp.stack(ksm), jnp.stack(vsm), jnp.stack(ssm))
```

```python
import functools
import math

import jax
import jax.numpy as jnp
from jax import lax
from jax.experimental import pallas as pl
from jax.experimental.pallas import tpu as pltpu

F32 = jnp.float32
BF16 = jnp.bfloat16

EPS = 1e-6
ROPE_BASE = 10000.0
RET_CHUNK = 128
N_MOD = 6
HEAD_W = 256
VMEM_LIMIT_BYTES = 60 * 1024 * 1024

TILES_PROMPT_WIDE = (2048, 512, 4096)
TILES_PROMPT_SQUARE = (2048, 256, 4096)
TILES_PROMPT_DEEP = (2048, 1024, 1024)
TILES_DECODE = (64, 1024, 4096)
PAGES_PER_STEP = 4

NT_DIMS = (((1,), (1,)), ((), ()))
TN_DIMS = (((0,), (0,)), ((), ()))


def _params(*sem):
    return pltpu.CompilerParams(dimension_semantics=sem, vmem_limit_bytes=VMEM_LIMIT_BYTES)


def _sigmoid(x):
    return 1.0 / (1.0 + jnp.exp(-x))


def _tile(n, want):
    t = min(n, want)
    while n % t:
        t -= 1
    return t


def _ada_kernel(c_ref, w_ref, b_ref, o_ref):
    c = c_ref[...]
    a = (c * _sigmoid(c)).astype(BF16)
    o_ref[...] = jnp.dot(a, w_ref[...].astype(BF16), preferred_element_type=F32) + b_ref[...]


def _ada(c_all, w_ada, b_ada):
    depth, d, n = w_ada.shape
    rows = c_all.shape[0]
    tn = _tile(n, 512)
    return pl.pallas_call(
        _ada_kernel,
        grid=(depth, n // tn),
        in_specs=[
            pl.BlockSpec((rows, d), lambda l, j: (0, 0)),
            pl.BlockSpec((None, d, tn), lambda l, j: (l, 0, j)),
            pl.BlockSpec((None, 1, tn), lambda l, j: (l, 0, j)),
        ],
        out_specs=pl.BlockSpec((None, rows, tn), lambda l, j: (l, 0, j)),
        out_shape=jax.ShapeDtypeStruct((depth, rows, n), F32),
        compiler_params=_params("parallel", "parallel"),
        name="ada",
    )(c_all, w_ada, b_ada.reshape(depth, 1, n))


def _norm_mod_kernel(x_ref, g_ref, sc_ref, sh_ref, o_ref):
    x = x_ref[...]
    y = x * lax.rsqrt(jnp.mean(x * x, axis=-1, keepdims=True) + EPS) * g_ref[...]
    o_ref[...] = (y * (1.0 + sc_ref[...]) + sh_ref[...]).astype(o_ref.dtype)


def _norm_mod(x, gain, mods, l, m_scale, m_shift):
    m, d = x.shape
    groups, r = mods.shape[2], mods.shape[3]
    tm = m // groups if r > 1 else _tile(m // groups, 256)
    per_group = (m // groups) // tm
    return pl.pallas_call(
        _norm_mod_kernel,
        grid=(m // tm,),
        in_specs=[
            pl.BlockSpec((tm, d), lambda i: (i, 0)),
            pl.BlockSpec((None, 1, d), lambda i: (l, 0, 0)),
            pl.BlockSpec((None, None, None, r, d), lambda i: (l, m_scale, i // per_group, 0, 0)),
            pl.BlockSpec((None, None, None, r, d), lambda i: (l, m_shift, i // per_group, 0, 0)),
        ],
        out_specs=pl.BlockSpec((tm, d), lambda i: (i, 0)),
        out_shape=jax.ShapeDtypeStruct((m, d), BF16),
        compiler_params=_params("parallel"),
        name="norm_mod",
    )(x, gain.reshape(gain.shape[0], 1, d), mods, mods)


def _norm_kernel(x_ref, g_ref, o_ref):
    x = x_ref[...]
    o_ref[...] = x * lax.rsqrt(jnp.mean(x * x, axis=-1, keepdims=True) + EPS) * g_ref[...]


def _final_norm(x, gain):
    m, d = x.shape
    tm = _tile(m, 256)
    return pl.pallas_call(
        _norm_kernel,
        grid=(m // tm,),
        in_specs=[pl.BlockSpec((tm, d), lambda i: (i, 0)), pl.BlockSpec((1, d), lambda i: (0, 0))],
        out_specs=pl.BlockSpec((tm, d), lambda i: (i, 0)),
        out_shape=jax.ShapeDtypeStruct((m, d), F32),
        compiler_params=_params("parallel"),
        name="final_norm",
    )(x, gain.reshape(1, d))


def _mm_kernel(*refs, nk, epilogue):
    if epilogue == "resid":
        a_ref, w_ref, x_ref, g_ref, o_ref = refs
    else:
        a_ref, w_ref, o_ref = refs
    part = jnp.dot(a_ref[...], w_ref[...].astype(BF16), preferred_element_type=F32)

    def finish(acc):
        if epilogue == "relu2":
            r = jnp.maximum(acc, 0.0)
            o_ref[...] = (r * r).astype(o_ref.dtype)
        elif epilogue == "resid":
            o_ref[...] = x_ref[...] + g_ref[...] * acc
        else:
            o_ref[...] = acc.astype(o_ref.dtype)

    if nk == 1:
        finish(part)
        return
    k = pl.program_id(2)

    @pl.when(k == 0)
    def _():
        o_ref[...] = part

    @pl.when(jnp.logical_and(k > 0, k < nk - 1))
    def _():
        o_ref[...] += part

    @pl.when(k == nk - 1)
    def _():
        finish(o_ref[...] + part)


def _matmul(a, w, l, n, out_dtype, tiles, epilogue="none", resid=None, mods=None, m_gate=None):
    m, kdim = a.shape
    groups = mods.shape[2] if epilogue == "resid" else 1
    tm = _tile(m // groups, tiles[0])
    tn = _tile(n, tiles[1])
    tk = _tile(kdim, tiles[2])
    nk = kdim // tk
    assert nk == 1 or out_dtype == F32
    once = pl.Buffered(1)
    in_specs = [
        pl.BlockSpec((tm, tk), lambda i, j, k: (i, k), pipeline_mode=once if nk == 1 else None),
        pl.BlockSpec((None, tk, tn), lambda i, j, k: (l, k, j)),
    ]
    args = [a, w]
    if epilogue == "resid":
        r = mods.shape[3]
        assert r == 1 or (groups == 1 and r == tm == m)
        per_group = (m // groups) // tm
        in_specs += [
            pl.BlockSpec((tm, tn), lambda i, j, k: (i, j), pipeline_mode=once if nk > 1 else None),
            pl.BlockSpec((None, None, None, r, tn), lambda i, j, k: (l, m_gate, i // per_group, 0, j)),
        ]
        args += [resid, mods]
    return pl.pallas_call(
        functools.partial(_mm_kernel, nk=nk, epilogue=epilogue),
        grid=(m // tm, n // tn, nk),
        in_specs=in_specs,
        out_specs=pl.BlockSpec((tm, tn), lambda i, j, k: (i, j)),
        out_shape=jax.ShapeDtypeStruct((m, n), out_dtype),
        compiler_params=_params("parallel", "parallel", "arbitrary"),
        name="mm_" + epilogue,
    )(*args)


def _mix_kernel(oa_ref, or_ref, wpa_ref, wpr_ref, ga_ref, gr_ref, o_ref):
    pa = jnp.dot(oa_ref[...], wpa_ref[...].astype(BF16), preferred_element_type=F32)
    pr = jnp.dot(or_ref[...], wpr_ref[...].astype(BF16), preferred_element_type=F32)
    o_ref[...] = (_sigmoid(ga_ref[...]) * pa + _sigmoid(gr_ref[...]) * pr).astype(o_ref.dtype)


def _mix(o_a, o_r, w_pa, w_pr, proj, l, col_ga, col_gr, d, tiles):
    m, ka = o_a.shape
    kr = o_r.shape[1]
    tm = _tile(m, tiles[0])
    tn = _tile(math.gcd(d, col_ga, col_gr), tiles[1])
    once = pl.Buffered(1)
    return pl.pallas_call(
        _mix_kernel,
        grid=(m // tm, d // tn),
        in_specs=[
            pl.BlockSpec((tm, ka), lambda i, j: (i, 0), pipeline_mode=once),
            pl.BlockSpec((tm, kr), lambda i, j: (i, 0), pipeline_mode=once),
            pl.BlockSpec((None, ka, tn), lambda i, j: (l, 0, j)),
            pl.BlockSpec((None, kr, tn), lambda i, j: (l, 0, j)),
            pl.BlockSpec((tm, tn), lambda i, j: (i, col_ga // tn + j)),
            pl.BlockSpec((tm, tn), lambda i, j: (i, col_gr // tn + j)),
        ],
        out_specs=pl.BlockSpec((tm, tn), lambda i, j: (i, j)),
        out_shape=jax.ShapeDtypeStruct((m, d), BF16),
        compiler_params=_params("parallel", "parallel"),
        name="mix",
    )(o_a, o_r, w_pa, w_pr, proj, proj)


def _lambda(lq1, lk1, lq2, lk2, lam_init):
    return (jnp.exp(jnp.sum(lq1[...] * lk1[...], axis=-1, keepdims=True))
            - jnp.exp(jnp.sum(lq2[...] * lk2[...], axis=-1, keepdims=True)) + lam_init)


def _subln(o, gain, lam_init):
    y = o * lax.rsqrt(jnp.mean(o * o, axis=-1, keepdims=True) + EPS) * gain
    return y * (1.0 - lam_init)


def _attn_prompt_kernel(q_ref, k_ref, v_ref, lq1, lk1, lq2, lk2, g_ref, o_ref, kb, vb, s_ref, acc_ref,
                        *, lam_init, t, tk, scale):
    qi = pl.program_id(2)
    half = HEAD_W // 2

    @pl.when(qi == 0)
    def _():
        kb[...] = k_ref[...].astype(BF16)
        vb[...] = v_ref[...].astype(BF16)

    lam = _lambda(lq1, lk1, lq2, lk2, lam_init)
    q = q_ref[...].astype(BF16)
    q1, q2 = q[:, :half], q[:, half:]

    def scores(j):
        kj = kb[pl.ds(pl.multiple_of(j * tk, tk), tk), :]
        s1 = lax.dot_general(q1, kj[:, :half], NT_DIMS, preferred_element_type=F32) * scale
        s2 = lax.dot_general(q2, kj[:, half:], NT_DIMS, preferred_element_type=F32) * scale
        return s1, s2

    def score_body(j, carry):
        m1, m2 = carry
        s1, s2 = scores(j)
        s_ref[0, j] = s1
        s_ref[1, j] = s2
        return (jnp.maximum(m1, jnp.max(s1, axis=-1, keepdims=True)),
                jnp.maximum(m2, jnp.max(s2, axis=-1, keepdims=True)))

    n_full = (qi * t) // tk
    neg = jnp.full((t, 1), -jnp.inf, F32)
    m1, m2 = lax.fori_loop(0, n_full, score_body, (neg, neg))
    s1, s2 = scores(n_full)
    q_pos = qi * t + lax.broadcasted_iota(jnp.int32, (t, tk), 0)
    k_pos = n_full * tk + lax.broadcasted_iota(jnp.int32, (t, tk), 1)
    keep = k_pos <= q_pos
    s1 = jnp.where(keep, s1, -jnp.inf)
    s2 = jnp.where(keep, s2, -jnp.inf)
    s_ref[0, n_full] = s1
    s_ref[1, n_full] = s2
    m1 = jnp.maximum(m1, jnp.max(s1, axis=-1, keepdims=True))
    m2 = jnp.maximum(m2, jnp.max(s2, axis=-1, keepdims=True))

    acc_ref[...] = jnp.zeros(acc_ref.shape, F32)

    def pv_body(j, carry):
        l1, l2 = carry
        e1 = jnp.exp(s_ref[0, j] - m1)
        e2 = jnp.exp(s_ref[1, j] - m2)
        vj = vb[pl.ds(pl.multiple_of(j * tk, tk), tk), :]
        acc_ref[0] += jnp.dot(e1.astype(BF16), vj, preferred_element_type=F32)
        acc_ref[1] += jnp.dot(e2.astype(BF16), vj, preferred_element_type=F32)
        return l1 + jnp.sum(e1, axis=-1, keepdims=True), l2 + jnp.sum(e2, axis=-1, keepdims=True)

    zero = jnp.zeros((t, 1), F32)
    l1, l2 = lax.fori_loop(0, n_full + 1, pv_body, (zero, zero))
    o = acc_ref[0] * (1.0 / l1) - acc_ref[1] * (lam / l2)
    o_ref[...] = _subln(o, g_ref[...], lam_init).astype(o_ref.dtype)


def _attn_prompt(proj, lams, subln, l, lam_init, bsz, seq, heads):
    m = proj.shape[0]
    tk = _tile(seq, 512)
    t = _tile(tk, 256)
    nq = seq // t
    lam_spec = pl.BlockSpec((None, 1, HEAD_W // 2), lambda b, h, i: (l, 0, 0))
    return pl.pallas_call(
        functools.partial(_attn_prompt_kernel, lam_init=lam_init, t=t, tk=tk, scale=(HEAD_W // 2) ** -0.5),
        grid=(bsz, heads, nq),
        in_specs=[
            pl.BlockSpec((t, HEAD_W), lambda b, h, i: (b * nq + i, h)),
            pl.BlockSpec((seq, HEAD_W), lambda b, h, i: (b, heads + h)),
            pl.BlockSpec((seq, HEAD_W), lambda b, h, i: (b, 2 * heads + h)),
            lam_spec, lam_spec, lam_spec, lam_spec,
            pl.BlockSpec((None, 1, HEAD_W), lambda b, h, i: (l, 0, 0)),
        ],
        out_specs=pl.BlockSpec((t, HEAD_W), lambda b, h, i: (b * nq + i, h)),
        out_shape=jax.ShapeDtypeStruct((m, heads * HEAD_W), BF16),
        scratch_shapes=[
            pltpu.VMEM((seq, HEAD_W), BF16),
            pltpu.VMEM((seq, HEAD_W), BF16),
            pltpu.VMEM((2, seq // tk, t, tk), F32),
            pltpu.VMEM((2, t, HEAD_W), F32),
        ],
        compiler_params=_params("parallel", "parallel", "arbitrary"),
        name="attn_prompt",
    )(proj, proj, proj, *lams, subln)


def _attn_decode_kernel(pt_ref, q_ref, kn_ref, vn_ref, *rest, lam_init, heads, n_new, n_pp, scale):
    kp_refs, vp_refs = rest[:n_pp], rest[n_pp:2 * n_pp]
    lq1, lk1, lq2, lk2, g_ref, o_ref, qall, bias_ref, m_ref, l_ref, acc_ref = rest[2 * n_pp:]
    p = pl.program_id(1)
    half = HEAD_W // 2
    grp = 2 * n_new
    rows = heads * grp
    page_rows = bias_ref.shape[1]

    def head_of_row(shape):
        return lax.broadcasted_iota(jnp.int32, shape, 0) // grp

    @pl.when(p == 0)
    def _():
        q = q_ref[...]
        z = jnp.zeros((n_new, half), F32)
        blocks = []
        for h in range(heads):
            q1 = q[:, h * HEAD_W:h * HEAD_W + half]
            q2 = q[:, h * HEAD_W + half:(h + 1) * HEAD_W]
            blocks.append(jnp.concatenate([q1, z], axis=1))
            blocks.append(jnp.concatenate([z, q2], axis=1))
        qall[...] = jnp.concatenate(blocks, axis=0).astype(BF16)
        shape = (rows, page_rows)
        same_head = lax.broadcasted_iota(jnp.int32, shape, 1) % heads == head_of_row(shape)
        bias_ref[...] = jnp.where(same_head, 0.0, -jnp.inf)
        m_ref[...] = jnp.full(m_ref.shape, -jnp.inf, F32)
        l_ref[...] = jnp.zeros(l_ref.shape, F32)
        acc_ref[...] = jnp.zeros(acc_ref.shape, F32)

    def absorb(keys, values, biases):
        qa = qall[...]
        ss = [lax.dot_general(qa, k.astype(BF16), NT_DIMS, preferred_element_type=F32) * scale + b
              for k, b in zip(keys, biases)]
        m_old = m_ref[...]
        m_new = m_old
        for s in ss:
            m_new = jnp.maximum(m_new, jnp.max(s, axis=-1, keepdims=True))
        alpha = jnp.exp(m_old - m_new)
        lsum = alpha * l_ref[...]
        acc = alpha * acc_ref[...]
        for s, v in zip(ss, values):
            e = jnp.exp(s - m_new)
            lsum = lsum + jnp.sum(e, axis=-1, keepdims=True)
            acc = acc + jnp.dot(e.astype(BF16), v.astype(BF16), preferred_element_type=F32)
        m_ref[...] = m_new
        l_ref[...] = lsum
        acc_ref[...] = acc

    bias = bias_ref[...]
    absorb([r[...] for r in kp_refs], [r[...] for r in vp_refs], [bias] * n_pp)

    @pl.when(p == pl.num_programs(1) - 1)
    def _():
        shape = (rows, n_new * heads)
        col = lax.broadcasted_iota(jnp.int32, shape, 1)
        tok = lax.broadcasted_iota(jnp.int32, shape, 0) % n_new
        ok = jnp.logical_and(col % heads == head_of_row(shape), col // heads <= tok)
        absorb([kn_ref[...]], [vn_ref[...]], [jnp.where(ok, 0.0, -jnp.inf)])
        lam = _lambda(lq1, lk1, lq2, lk2, lam_init)
        acc = acc_ref[...]
        lsum = l_ref[...]
        outs = []
        for h in range(heads):
            r0 = h * grp
            o = (acc[r0:r0 + n_new] / lsum[r0:r0 + n_new]
                 - lam * (acc[r0 + n_new:r0 + grp] / lsum[r0 + n_new:r0 + grp]))
            outs.append(_subln(o, g_ref[...], lam_init))
        o_ref[...] = jnp.concatenate(outs, axis=1).astype(o_ref.dtype)


def _attn_decode(proj, k_new, v_new, cache_k, cache_v, page_table, lams, subln, l, lam_init, bsz, n_new, heads):
    depth, n_pool, page = cache_k.shape[:3]
    width = heads * HEAD_W
    n_pages = page_table.shape[1]
    n_pp = _tile(n_pages, PAGES_PER_STEP)
    page_rows = page * heads
    ck = cache_k.reshape(depth, n_pool, page_rows, HEAD_W)
    cv = cache_v.reshape(depth, n_pool, page_rows, HEAD_W)
    kn = k_new.reshape(bsz, n_new * heads, HEAD_W)
    vn = v_new.reshape(bsz, n_new * heads, HEAD_W)
    lam_spec = pl.BlockSpec((None, 1, HEAD_W // 2), lambda b, p, pt: (l, 0, 0))
    page_specs = [pl.BlockSpec((None, None, page_rows, HEAD_W), lambda b, p, pt, i=i: (l, pt[b, p * n_pp + i], 0, 0))
                  for i in range(n_pp)]
    new_spec = pl.BlockSpec((None, n_new * heads, HEAD_W), lambda b, p, pt: (b, 0, 0))
    rows = heads * 2 * n_new
    grid_spec = pltpu.PrefetchScalarGridSpec(
        num_scalar_prefetch=1,
        grid=(bsz, n_pages // n_pp),
        in_specs=[pl.BlockSpec((n_new, width), lambda b, p, pt: (b, 0)), new_spec, new_spec]
        + page_specs + page_specs
        + [lam_spec, lam_spec, lam_spec, lam_spec, pl.BlockSpec((None, 1, HEAD_W), lambda b, p, pt: (l, 0, 0))],
        out_specs=pl.BlockSpec((None, n_new, width), lambda b, p, pt: (b, 0, 0)),
        scratch_shapes=[
            pltpu.VMEM((rows, HEAD_W), BF16),
            pltpu.VMEM((rows, page_rows), F32),
            pltpu.VMEM((rows, 1), F32),
            pltpu.VMEM((rows, 1), F32),
            pltpu.VMEM((rows, HEAD_W), F32),
        ],
    )
    out = pl.pallas_call(
        functools.partial(_attn_decode_kernel, lam_init=lam_init, heads=heads, n_new=n_new, n_pp=n_pp,
                          scale=(HEAD_W // 2) ** -0.5),
        grid_spec=grid_spec,
        out_shape=jax.ShapeDtypeStruct((bsz, n_new, width), BF16),
        compiler_params=_params("parallel", "arbitrary"),
        name="attn_decode",
    )(page_table, proj, kn, vn, *([ck] * n_pp), *([cv] * n_pp), *lams, subln)
    return out.reshape(bsz * n_new, width)


def _retention_kernel(*refs, nb, seq, chunk, has_state, mxu_dtype):
    if has_state:
        q_ref, k_ref, v_ref, g_ref, cos_ref, sin_ref, dm_ref, qd_ref, kd_ref, cd_ref, s0_ref, o_ref, s_ref = refs
    else:
        q_ref, k_ref, v_ref, g_ref, cos_ref, sin_ref, dm_ref, qd_ref, kd_ref, cd_ref, o_ref, s_ref = refs
    half = HEAD_W // 2
    n_chunks = seq // chunk
    k_scale = HEAD_W ** -0.5
    dmask = dm_ref[...]
    qdec = qd_ref[...]
    kdec = kd_ref[...]
    cdec = cd_ref[...]

    def rotate(x, cos, sin):
        x1, x2 = x[:, :half], x[:, half:]
        return jnp.concatenate([x1 * cos - x2 * sin, x2 * cos + x1 * sin], axis=1)

    def chunk_out(rows, pos, state):
        cos = cos_ref[pos, :]
        sin = sin_ref[pos, :]
        qc = rotate(q_ref[rows, :], cos, sin).astype(mxu_dtype)
        kf = rotate(k_ref[rows, :], cos, sin) * k_scale
        vc = v_ref[rows, :].astype(mxu_dtype)
        scores = lax.dot_general(qc, kf.astype(mxu_dtype), NT_DIMS, preferred_element_type=F32) * dmask
        o = (jnp.dot(scores.astype(mxu_dtype), vc, preferred_element_type=F32)
             + jnp.dot(qc, state.astype(mxu_dtype), preferred_element_type=F32) * qdec)
        new_state = state * cdec + lax.dot_general((kf * kdec).astype(mxu_dtype), vc, TN_DIMS,
                                                   preferred_element_type=F32)
        o = o * lax.rsqrt(jnp.mean(o * o, axis=-1, keepdims=True) + EPS)
        g = g_ref[rows, :]
        return o * (g * _sigmoid(g)), new_state

    for b in range(nb):
        state0 = s0_ref[b] if has_state else jnp.zeros((HEAD_W, HEAD_W), F32)
        if n_chunks == 1:
            out, state = chunk_out(pl.ds(b * seq, chunk), pl.ds(0, chunk), state0)
            o_ref[pl.ds(b * seq, chunk), :] = out.astype(o_ref.dtype)
        else:
            def body(c, state, b=b):
                start = pl.multiple_of(c * chunk, chunk)
                out, state = chunk_out(pl.ds(b * seq + start, chunk), pl.ds(start, chunk), state)
                o_ref[pl.ds(b * seq + start, chunk), :] = out.astype(o_ref.dtype)
                return state
            state = lax.fori_loop(0, n_chunks, body, state0)
        s_ref[b] = state


def _retention(proj, state0, l, tables, bsz, seq, heads, col_q, nb, out_dtype, mxu_dtype):
    cos, sin, dmask, qdec, kdec, cdec = tables
    m = proj.shape[0]
    chunk = dmask.shape[1]
    cb = col_q // HEAD_W
    rows = nb * seq

    def col(which):
        return pl.BlockSpec((rows, HEAD_W), lambda g, h: (g, cb + which * heads + h))

    in_specs = [
        col(0), col(1), col(2), col(3),
        pl.BlockSpec((seq, HEAD_W // 2), lambda g, h: (0, 0)),
        pl.BlockSpec((seq, HEAD_W // 2), lambda g, h: (0, 0)),
        pl.BlockSpec((None, chunk, chunk), lambda g, h: (h, 0, 0)),
        pl.BlockSpec((None, chunk, 1), lambda g, h: (h, 0, 0)),
        pl.BlockSpec((None, chunk, 1), lambda g, h: (h, 0, 0)),
        pl.BlockSpec((None, 1, 1), lambda g, h: (h, 0, 0)),
    ]
    args = [proj, proj, proj, proj, cos, sin, dmask, qdec, kdec, cdec]
    has_state = state0 is not None
    if has_state:
        in_specs.append(pl.BlockSpec((None, nb, None, HEAD_W, HEAD_W), lambda g, h: (l, g, h, 0, 0)))
        args.append(state0)
    return pl.pallas_call(
        functools.partial(_retention_kernel, nb=nb, seq=seq, chunk=chunk, has_state=has_state, mxu_dtype=mxu_dtype),
        grid=(bsz // nb, heads),
        in_specs=in_specs,
        out_specs=[
            pl.BlockSpec((rows, HEAD_W), lambda g, h: (g, h)),
            pl.BlockSpec((nb, None, HEAD_W, HEAD_W), lambda g, h: (g, h, 0, 0)),
        ],
        out_shape=[
            jax.ShapeDtypeStruct((m, heads * HEAD_W), out_dtype),
            jax.ShapeDtypeStruct((bsz, heads, HEAD_W, HEAD_W), F32),
        ],
        compiler_params=_params("parallel", "parallel"),
        name="retention",
    )(*args)


def _retention_tables(seq, first_pos, heads):
    half = HEAD_W // 2
    inv = 1.0 / (ROPE_BASE ** jnp.linspace(0.0, 1.0, half, dtype=F32))
    pos = (first_pos + jnp.arange(seq, dtype=jnp.int32)).astype(F32)
    ang = pos[:, None] * inv[None, :]
    chunk = seq if seq <= RET_CHUNK else RET_CHUNK
    lg = jnp.log(1.0 - 2.0 ** (-5.0 - jnp.arange(heads, dtype=F32)))
    idx = jnp.arange(chunk, dtype=F32)
    rel = idx[:, None] - idx[None, :]
    causal = rel >= 0
    dmask = jnp.where(causal[None], jnp.exp(jnp.where(causal, rel, 0.0)[None] * lg[:, None, None]), 0.0)
    qdec = jnp.exp((idx + 1.0)[None, :] * lg[:, None])[:, :, None]
    kdec = jnp.exp((chunk - 1.0 - idx)[None, :] * lg[:, None])[:, :, None]
    cdec = jnp.exp(chunk * lg)[:, None, None]
    return jnp.cos(ang), jnp.sin(ang), dmask, qdec, kdec, cdec


def kernel(x_prompt, x_sample, cache_k, cache_v, state_ret, page_table, c_prompt, c_sample, w_ada, b_ada, norm1, norm2, w_in, lambda_q1, lambda_k1, lambda_q2, lambda_k2, da_subln, w_pa, w_pr, w_o, w_up, w_down, norm_f):
    bp, seq, d = x_prompt.shape
    bs, n_new, _ = x_sample.shape
    depth = w_in.shape[0]
    page, heads = cache_k.shape[2], cache_k.shape[3]
    r_heads = state_ret.shape[2]
    assert cache_k.shape[4] == cache_v.shape[4] == state_ret.shape[3] == state_ret.shape[4] == HEAD_W
    assert heads == r_heads
    past_len = page_table.shape[1] * page
    d_ff = w_up.shape[2]
    hw = heads * HEAD_W
    col_k, col_v, col_qr = hw, 2 * hw, 3 * hw
    col_ga = 7 * hw
    col_gr = col_ga + d

    c_rows = -(-(bp + bs) // 8) * 8
    c_all = jnp.concatenate([c_prompt, c_sample, jnp.zeros((c_rows - bp - bs, d), F32)], axis=0)
    mod = _ada(c_all, w_ada, b_ada)
    mod_p = mod[:, :bp].reshape(depth, bp, N_MOD, 1, d).transpose(0, 2, 1, 3, 4)
    mod_s = mod[:, bp:bp + bs].reshape(depth, bs, N_MOD, d).transpose(0, 2, 1, 3)
    mod_s = jnp.repeat(mod_s, n_new, axis=2)[:, :, None]

    lams = [a.reshape(depth, 1, -1) for a in (lambda_q1, lambda_k1, lambda_q2, lambda_k2)]
    subln = da_subln.reshape(depth, 1, HEAD_W)
    tab_p = _retention_tables(seq, 0, heads)
    tab_s = _retention_tables(n_new, past_len, heads)

    def dense_tail(x, mods, l, proj, o_a, o_r, decode):
        wide, square, deep = (TILES_DECODE,) * 3 if decode else (TILES_PROMPT_WIDE, TILES_PROMPT_SQUARE, TILES_PROMPT_DEEP)
        mixed = _mix(o_a, o_r, w_pa, w_pr, proj, l, col_ga, col_gr, d, square)
        x = _matmul(mixed, w_o, l, d, F32, square, "resid", x, mods, 2)
        h2 = _norm_mod(x, norm2, mods, l, 4, 3)
        u = _matmul(h2, w_up, l, d_ff, BF16, wide, "relu2")
        return _matmul(u, w_down, l, d, F32, deep, "resid", x, mods, 5)

    xp = x_prompt.reshape(bp * seq, d)
    xs = x_sample.reshape(bs * n_new, d)
    kp, vp, sp, ksm, vsm, ssm = [], [], [], [], [], []
    for l in range(depth):
        lam_init = 0.8 - 0.6 * math.exp(-0.3 * l)

        h = _norm_mod(xp, norm1, mod_p, l, 1, 0)
        proj = _matmul(h, w_in, l, w_in.shape[2], F32, TILES_PROMPT_WIDE)
        o_a = _attn_prompt(proj, lams, subln, l, lam_init, bp, seq, heads)
        o_r, s_new = _retention(proj, None, l, tab_p, bp, seq, heads, col_qr, 1, BF16, BF16)
        kp.append(proj[:, col_k:col_k + hw].reshape(bp, seq, heads, HEAD_W))
        vp.append(proj[:, col_v:col_v + hw].reshape(bp, seq, heads, HEAD_W))
        sp.append(s_new)
        xp = dense_tail(xp, mod_p, l, proj, o_a, o_r, False)

        h = _norm_mod(xs, norm1, mod_s, l, 1, 0)
        proj = _matmul(h, w_in, l, w_in.shape[2], F32, TILES_DECODE)
        k_new = proj[:, col_k:col_k + hw].reshape(bs, n_new, heads, HEAD_W)
        v_new = proj[:, col_v:col_v + hw].reshape(bs, n_new, heads, HEAD_W)
        o_a = _attn_decode(proj, k_new, v_new, cache_k, cache_v, page_table, lams, subln, l, lam_init,
                           bs, n_new, heads)
        o_r, s_new = _retention(proj, state_ret, l, tab_s, bs, n_new, heads, col_qr, bs, F32, F32)
        ksm.append(k_new)
        vsm.append(v_new)
        ssm.append(s_new)
        xs = dense_tail(xs, mod_s, l, proj, o_a, o_r.astype(BF16), True)

    y_prompt = _final_norm(xp, norm_f).reshape(bp, seq, d)
    y_sample = _final_norm(xs, norm_f).reshape(bs, n_new, d)
    return (y_prompt, y_sample, jnp.stack(kp), jnp.stack(vp), jnp.stack(sp),
            jnp.stack(ksm), jnp.stack(vsm), jnp.stack(ssm))
```

```python
import functools
import math

import jax
import jax.numpy as jnp
from jax import lax
from jax.experimental import pallas as pl
from jax.experimental.pallas import tpu as pltpu

F32 = jnp.float32
BF16 = jnp.bfloat16

EPS = 1e-6
ROPE_BASE = 10000.0
RET_CHUNK = 128
N_MOD = 6
HEAD_W = 256
VMEM_LIMIT_BYTES = 60 * 1024 * 1024

TILES_PROMPT_WIDE = (2048, 512, 4096)
TILES_PROMPT_SQUARE = (1024, 512, 4096)
TILES_PROMPT_DEEP = (2048, 1024, 1024)
TILES_DECODE = (64, 1024, 4096)
MXU_COLS = 256
PAGES_PER_STEP = 4

NT_DIMS = (((1,), (1,)), ((), ()))
TN_DIMS = (((0,), (0,)), ((), ()))


def _params(*sem):
    return pltpu.CompilerParams(dimension_semantics=sem, vmem_limit_bytes=VMEM_LIMIT_BYTES)


def _sigmoid(x):
    return 1.0 / (1.0 + jnp.exp(-x))


def _tile(n, want):
    t = min(n, want)
    while n % t:
        t -= 1
    return t


def _ada_kernel(c_ref, w_ref, b_ref, o_ref):
    c = c_ref[...]
    a = (c * _sigmoid(c)).astype(BF16)
    o_ref[...] = jnp.dot(a, w_ref[...].astype(BF16), preferred_element_type=F32) + b_ref[...]


def _ada(c_all, w_ada, b_ada):
    depth, d, n = w_ada.shape
    rows = c_all.shape[0]
    tn = _tile(n, 512)
    return pl.pallas_call(
        _ada_kernel,
        grid=(depth, n // tn),
        in_specs=[
            pl.BlockSpec((rows, d), lambda l, j: (0, 0)),
            pl.BlockSpec((None, d, tn), lambda l, j: (l, 0, j)),
            pl.BlockSpec((None, 1, tn), lambda l, j: (l, 0, j)),
        ],
        out_specs=pl.BlockSpec((None, rows, tn), lambda l, j: (l, 0, j)),
        out_shape=jax.ShapeDtypeStruct((depth, rows, n), F32),
        compiler_params=_params("parallel", "parallel"),
        name="ada",
    )(c_all, w_ada, b_ada.reshape(depth, 1, n))


def _norm_mod_kernel(x_ref, g_ref, sc_ref, sh_ref, o_ref):
    x = x_ref[...]
    y = x * lax.rsqrt(jnp.mean(x * x, axis=-1, keepdims=True) + EPS) * g_ref[...]
    o_ref[...] = (y * (1.0 + sc_ref[...]) + sh_ref[...]).astype(o_ref.dtype)


def _norm_mod(x, gain, mods, l, m_scale, m_shift):
    m, d = x.shape
    groups, r = mods.shape[2], mods.shape[3]
    tm = m // groups if r > 1 else _tile(m // groups, 512)
    per_group = (m // groups) // tm
    return pl.pallas_call(
        _norm_mod_kernel,
        grid=(m // tm,),
        in_specs=[
            pl.BlockSpec((tm, d), lambda i: (i, 0)),
            pl.BlockSpec((None, 1, d), lambda i: (l, 0, 0)),
            pl.BlockSpec((None, None, None, r, d), lambda i: (l, m_scale, i // per_group, 0, 0)),
            pl.BlockSpec((None, None, None, r, d), lambda i: (l, m_shift, i // per_group, 0, 0)),
        ],
        out_specs=pl.BlockSpec((tm, d), lambda i: (i, 0)),
        out_shape=jax.ShapeDtypeStruct((m, d), BF16),
        compiler_params=_params("parallel"),
        name="norm_mod",
    )(x, gain.reshape(gain.shape[0], 1, d), mods, mods)


def _norm_kernel(x_ref, g_ref, o_ref):
    x = x_ref[...]
    o_ref[...] = x * lax.rsqrt(jnp.mean(x * x, axis=-1, keepdims=True) + EPS) * g_ref[...]


def _final_norm(x, gain):
    m, d = x.shape
    tm = _tile(m, 256)
    return pl.pallas_call(
        _norm_kernel,
        grid=(m // tm,),
        in_specs=[pl.BlockSpec((tm, d), lambda i: (i, 0)), pl.BlockSpec((1, d), lambda i: (0, 0))],
        out_specs=pl.BlockSpec((tm, d), lambda i: (i, 0)),
        out_shape=jax.ShapeDtypeStruct((m, d), F32),
        compiler_params=_params("parallel"),
        name="final_norm",
    )(x, gain.reshape(1, d))


def _mm_kernel(*refs, nk, epilogue):
    if epilogue == "resid":
        a_ref, w_ref, x_ref, g_ref, o_ref = refs
    else:
        a_ref, w_ref, o_ref = refs
    tn = o_ref.shape[1]
    sub = _tile(tn, MXU_COLS) if nk > 1 else tn

    def run(first, last):
        for c in range(tn // sub):
            cols = slice(c * sub, (c + 1) * sub)
            acc = jnp.dot(a_ref[...], w_ref[:, cols].astype(BF16), preferred_element_type=F32)
            if not first:
                acc = o_ref[:, cols] + acc
            if not last:
                o_ref[:, cols] = acc
            elif epilogue == "relu2":
                r = jnp.maximum(acc, 0.0)
                o_ref[:, cols] = (r * r).astype(o_ref.dtype)
            elif epilogue == "resid":
                o_ref[:, cols] = x_ref[:, cols] + g_ref[:, cols] * acc
            else:
                o_ref[:, cols] = acc.astype(o_ref.dtype)

    if nk == 1:
        run(True, True)
        return
    k = pl.program_id(2)
    pl.when(k == 0)(lambda: run(True, False))
    pl.when(jnp.logical_and(k > 0, k < nk - 1))(lambda: run(False, False))
    pl.when(k == nk - 1)(lambda: run(False, True))


def _matmul(a, w, l, n, out_dtype, tiles, epilogue="none", resid=None, mods=None, m_gate=None, col0=0):
    m, kdim = a.shape
    groups = mods.shape[2] if epilogue == "resid" else 1
    tm = _tile(m // groups, tiles[0])
    tn = _tile(math.gcd(n, col0) if col0 else n, tiles[1])
    tk = _tile(kdim, tiles[2])
    nk = kdim // tk
    jb = col0 // tn
    assert nk == 1 or out_dtype == F32
    once = pl.Buffered(1)
    in_specs = [
        pl.BlockSpec((tm, tk), lambda i, j, k: (i, k), pipeline_mode=once if nk == 1 else None),
        pl.BlockSpec((None, tk, tn), lambda i, j, k: (l, k, jb + j)),
    ]
    args = [a, w]
    if epilogue == "resid":
        r = mods.shape[3]
        assert r == 1 or (groups == 1 and r == tm == m)
        per_group = (m // groups) // tm
        in_specs += [
            pl.BlockSpec((tm, tn), lambda i, j, k: (i, j), pipeline_mode=once if nk > 1 else None),
            pl.BlockSpec((None, None, None, r, tn), lambda i, j, k: (l, m_gate, i // per_group, 0, j)),
        ]
        args += [resid, mods]
    return pl.pallas_call(
        functools.partial(_mm_kernel, nk=nk, epilogue=epilogue),
        grid=(m // tm, n // tn, nk),
        in_specs=in_specs,
        out_specs=pl.BlockSpec((tm, tn), lambda i, j, k: (i, j)),
        out_shape=jax.ShapeDtypeStruct((m, n), out_dtype),
        compiler_params=_params("parallel", "parallel", "arbitrary"),
        name="mm_" + epilogue,
    )(*args)


def _mix_kernel(oa_ref, or_ref, wpa_ref, wpr_ref, ga_ref, gr_ref, o_ref):
    pa = jnp.dot(oa_ref[...], wpa_ref[...].astype(BF16), preferred_element_type=F32)
    pr = jnp.dot(or_ref[...], wpr_ref[...].astype(BF16), preferred_element_type=F32)
    o_ref[...] = (_sigmoid(ga_ref[...]) * pa + _sigmoid(gr_ref[...]) * pr).astype(o_ref.dtype)


def _mix(o_a, o_r, w_pa, w_pr, proj, l, col_ga, col_gr, d, tiles):
    m, ka = o_a.shape
    kr = o_r.shape[1]
    tm = _tile(m, tiles[0])
    tn = _tile(math.gcd(d, col_ga, col_gr), tiles[1])
    once = pl.Buffered(1)
    return pl.pallas_call(
        _mix_kernel,
        grid=(m // tm, d // tn),
        in_specs=[
            pl.BlockSpec((tm, ka), lambda i, j: (i, 0), pipeline_mode=once),
            pl.BlockSpec((tm, kr), lambda i, j: (i, 0), pipeline_mode=once),
            pl.BlockSpec((None, ka, tn), lambda i, j: (l, 0, j)),
            pl.BlockSpec((None, kr, tn), lambda i, j: (l, 0, j)),
            pl.BlockSpec((tm, tn), lambda i, j: (i, col_ga // tn + j)),
            pl.BlockSpec((tm, tn), lambda i, j: (i, col_gr // tn + j)),
        ],
        out_specs=pl.BlockSpec((tm, tn), lambda i, j: (i, j)),
        out_shape=jax.ShapeDtypeStruct((m, d), BF16),
        compiler_params=_params("parallel", "parallel"),
        name="mix",
    )(o_a, o_r, w_pa, w_pr, proj, proj)


def _lambda(lq1, lk1, lq2, lk2, lam_init):
    return (jnp.exp(jnp.sum(lq1[...] * lk1[...], axis=-1, keepdims=True))
            - jnp.exp(jnp.sum(lq2[...] * lk2[...], axis=-1, keepdims=True)) + lam_init)


def _subln(o, gain, lam_init):
    y = o * lax.rsqrt(jnp.mean(o * o, axis=-1, keepdims=True) + EPS) * gain
    return y * (1.0 - lam_init)


def _attn_prompt_kernel(q_ref, k_ref, v_ref, lq1, lk1, lq2, lk2, g_ref, o_ref, kb, vb, s_ref, acc_ref,
                        *, lam_init, t, tk, scale):
    qi = pl.program_id(2)
    half = HEAD_W // 2

    @pl.when(qi == 0)
    def _():
        kb[...] = k_ref[...].astype(BF16)
        vb[...] = v_ref[...].astype(BF16)

    lam = _lambda(lq1, lk1, lq2, lk2, lam_init)
    q = q_ref[...].astype(BF16)
    q1, q2 = q[:, :half], q[:, half:]

    def scores(j):
        kj = kb[pl.ds(pl.multiple_of(j * tk, tk), tk), :]
        s1 = lax.dot_general(q1, kj[:, :half], NT_DIMS, preferred_element_type=F32) * scale
        s2 = lax.dot_general(q2, kj[:, half:], NT_DIMS, preferred_element_type=F32) * scale
        return s1, s2

    def score_body(j, carry):
        m1, m2 = carry
        s1, s2 = scores(j)
        s_ref[0, j] = s1
        s_ref[1, j] = s2
        return (jnp.maximum(m1, jnp.max(s1, axis=-1, keepdims=True)),
                jnp.maximum(m2, jnp.max(s2, axis=-1, keepdims=True)))

    n_full = (qi * t) // tk
    neg = jnp.full((t, 1), -jnp.inf, F32)
    m1, m2 = lax.fori_loop(0, n_full, score_body, (neg, neg))
    s1, s2 = scores(n_full)
    q_pos = qi * t + lax.broadcasted_iota(jnp.int32, (t, tk), 0)
    k_pos = n_full * tk + lax.broadcasted_iota(jnp.int32, (t, tk), 1)
    keep = k_pos <= q_pos
    s1 = jnp.where(keep, s1, -jnp.inf)
    s2 = jnp.where(keep, s2, -jnp.inf)
    s_ref[0, n_full] = s1
    s_ref[1, n_full] = s2
    m1 = jnp.maximum(m1, jnp.max(s1, axis=-1, keepdims=True))
    m2 = jnp.maximum(m2, jnp.max(s2, axis=-1, keepdims=True))

    acc_ref[...] = jnp.zeros(acc_ref.shape, F32)

    def pv_body(j, carry):
        l1, l2 = carry
        e1 = jnp.exp(s_ref[0, j] - m1)
        e2 = jnp.exp(s_ref[1, j] - m2)
        vj = vb[pl.ds(pl.multiple_of(j * tk, tk), tk), :]
        acc_ref[0] += jnp.dot(e1.astype(BF16), vj, preferred_element_type=F32)
        acc_ref[1] += jnp.dot(e2.astype(BF16), vj, preferred_element_type=F32)
        return l1 + jnp.sum(e1, axis=-1, keepdims=True), l2 + jnp.sum(e2, axis=-1, keepdims=True)

    zero = jnp.zeros((t, 1), F32)
    l1, l2 = lax.fori_loop(0, n_full + 1, pv_body, (zero, zero))
    o = acc_ref[0] * (1.0 / l1) - acc_ref[1] * (lam / l2)
    o_ref[...] = _subln(o, g_ref[...], lam_init).astype(o_ref.dtype)


def _attn_prompt(q, k, v, lams, subln, l, lam_init, bsz, seq, heads):
    m = q.shape[0]
    tk = _tile(seq, 512)
    t = tk
    nq = seq // t
    lam_spec = pl.BlockSpec((None, 1, HEAD_W // 2), lambda b, h, i: (l, 0, 0))
    return pl.pallas_call(
        functools.partial(_attn_prompt_kernel, lam_init=lam_init, t=t, tk=tk, scale=(HEAD_W // 2) ** -0.5),
        grid=(bsz, heads, nq),
        in_specs=[
            pl.BlockSpec((t, HEAD_W), lambda b, h, i: (b * nq + i, h)),
            pl.BlockSpec((seq, HEAD_W), lambda b, h, i: (b, h)),
            pl.BlockSpec((seq, HEAD_W), lambda b, h, i: (b, h)),
            lam_spec, lam_spec, lam_spec, lam_spec,
            pl.BlockSpec((None, 1, HEAD_W), lambda b, h, i: (l, 0, 0)),
        ],
        out_specs=pl.BlockSpec((t, HEAD_W), lambda b, h, i: (b * nq + i, h)),
        out_shape=jax.ShapeDtypeStruct((m, heads * HEAD_W), BF16),
        scratch_shapes=[
            pltpu.VMEM((seq, HEAD_W), BF16),
            pltpu.VMEM((seq, HEAD_W), BF16),
            pltpu.VMEM((2, seq // tk, t, tk), F32),
            pltpu.VMEM((2, t, HEAD_W), F32),
        ],
        compiler_params=_params("parallel", "parallel", "arbitrary"),
        name="attn_prompt",
    )(q, k, v, *lams, subln)


def _attn_decode_kernel(pt_ref, q_ref, kn_ref, vn_ref, *rest, lam_init, heads, n_new, n_pp, scale):
    kp_refs, vp_refs = rest[:n_pp], rest[n_pp:2 * n_pp]
    lq1, lk1, lq2, lk2, g_ref, o_ref, qall, bias_ref, m_ref, l_ref, acc_ref = rest[2 * n_pp:]
    p = pl.program_id(1)
    half = HEAD_W // 2
    grp = 2 * n_new
    rows = heads * grp
    page_rows = bias_ref.shape[1]

    def head_of_row(shape):
        return lax.broadcasted_iota(jnp.int32, shape, 0) // grp

    @pl.when(p == 0)
    def _():
        q = q_ref[...]
        z = jnp.zeros((n_new, half), F32)
        blocks = []
        for h in range(heads):
            q1 = q[:, h * HEAD_W:h * HEAD_W + half]
            q2 = q[:, h * HEAD_W + half:(h + 1) * HEAD_W]
            blocks.append(jnp.concatenate([q1, z], axis=1))
            blocks.append(jnp.concatenate([z, q2], axis=1))
        qall[...] = jnp.concatenate(blocks, axis=0).astype(BF16)
        shape = (rows, page_rows)
        same_head = lax.broadcasted_iota(jnp.int32, shape, 1) % heads == head_of_row(shape)
        bias_ref[...] = jnp.where(same_head, 0.0, -jnp.inf)
        m_ref[...] = jnp.full(m_ref.shape, -jnp.inf, F32)
        l_ref[...] = jnp.zeros(l_ref.shape, F32)
        acc_ref[...] = jnp.zeros(acc_ref.shape, F32)

    def absorb(keys, values, biases):
        qa = qall[...]
        ss = [lax.dot_general(qa, k.astype(BF16), NT_DIMS, preferred_element_type=F32) * scale + b
              for k, b in zip(keys, biases)]
        m_old = m_ref[...]
        m_new = m_old
        for s in ss:
            m_new = jnp.maximum(m_new, jnp.max(s, axis=-1, keepdims=True))
        alpha = jnp.exp(m_old - m_new)
        lsum = alpha * l_ref[...]
        acc = alpha * acc_ref[...]
        for s, v in zip(ss, values):
            e = jnp.exp(s - m_new)
            lsum = lsum + jnp.sum(e, axis=-1, keepdims=True)
            acc = acc + jnp.dot(e.astype(BF16), v.astype(BF16), preferred_element_type=F32)
        m_ref[...] = m_new
        l_ref[...] = lsum
        acc_ref[...] = acc

    bias = bias_ref[...]
    absorb([r[...] for r in kp_refs], [r[...] for r in vp_refs], [bias] * n_pp)

    @pl.when(p == pl.num_programs(1) - 1)
    def _():
        shape = (rows, n_new * heads)
        col = lax.broadcasted_iota(jnp.int32, shape, 1)
        tok = lax.broadcasted_iota(jnp.int32, shape, 0) % n_new
        ok = jnp.logical_and(col % heads == head_of_row(shape), col // heads <= tok)
        absorb([kn_ref[...]], [vn_ref[...]], [jnp.where(ok, 0.0, -jnp.inf)])
        lam = _lambda(lq1, lk1, lq2, lk2, lam_init)
        acc = acc_ref[...]
        lsum = l_ref[...]
        outs = []
        for h in range(heads):
            r0 = h * grp
            o = (acc[r0:r0 + n_new] / lsum[r0:r0 + n_new]
                 - lam * (acc[r0 + n_new:r0 + grp] / lsum[r0 + n_new:r0 + grp]))
            outs.append(_subln(o, g_ref[...], lam_init))
        o_ref[...] = jnp.concatenate(outs, axis=1).astype(o_ref.dtype)


def _attn_decode(proj, k_new, v_new, cache_k, cache_v, page_table, lams, subln, l, lam_init, bsz, n_new, heads):
    depth, n_pool, page = cache_k.shape[:3]
    width = heads * HEAD_W
    n_pages = page_table.shape[1]
    n_pp = _tile(n_pages, PAGES_PER_STEP)
    page_rows = page * heads
    ck = cache_k.reshape(depth, n_pool, page_rows, HEAD_W)
    cv = cache_v.reshape(depth, n_pool, page_rows, HEAD_W)
    kn = k_new.reshape(bsz, n_new * heads, HEAD_W)
    vn = v_new.reshape(bsz, n_new * heads, HEAD_W)
    lam_spec = pl.BlockSpec((None, 1, HEAD_W // 2), lambda b, p, pt: (l, 0, 0))
    page_specs = [pl.BlockSpec((None, None, page_rows, HEAD_W), lambda b, p, pt, i=i: (l, pt[b, p * n_pp + i], 0, 0))
                  for i in range(n_pp)]
    new_spec = pl.BlockSpec((None, n_new * heads, HEAD_W), lambda b, p, pt: (b, 0, 0))
    rows = heads * 2 * n_new
    grid_spec = pltpu.PrefetchScalarGridSpec(
        num_scalar_prefetch=1,
        grid=(bsz, n_pages // n_pp),
        in_specs=[pl.BlockSpec((n_new, width), lambda b, p, pt: (b, 0)), new_spec, new_spec]
        + page_specs + page_specs
        + [lam_spec, lam_spec, lam_spec, lam_spec, pl.BlockSpec((None, 1, HEAD_W), lambda b, p, pt: (l, 0, 0))],
        out_specs=pl.BlockSpec((None, n_new, width), lambda b, p, pt: (b, 0, 0)),
        scratch_shapes=[
            pltpu.VMEM((rows, HEAD_W), BF16),
            pltpu.VMEM((rows, page_rows), F32),
            pltpu.VMEM((rows, 1), F32),
            pltpu.VMEM((rows, 1), F32),
            pltpu.VMEM((rows, HEAD_W), F32),
        ],
    )
    out = pl.pallas_call(
        functools.partial(_attn_decode_kernel, lam_init=lam_init, heads=heads, n_new=n_new, n_pp=n_pp,
                          scale=(HEAD_W // 2) ** -0.5),
        grid_spec=grid_spec,
        out_shape=jax.ShapeDtypeStruct((bsz, n_new, width), BF16),
        compiler_params=_params("parallel", "arbitrary"),
        name="attn_decode",
    )(page_table, proj, kn, vn, *([ck] * n_pp), *([cv] * n_pp), *lams, subln)
    return out.reshape(bsz * n_new, width)


def _retention_kernel(*refs, nb, seq, chunk, has_state, mxu_dtype):
    if has_state:
        q_ref, k_ref, v_ref, g_ref, cos_ref, sin_ref, dm_ref, qd_ref, kd_ref, cd_ref, s0_ref, o_ref, s_ref = refs
    else:
        q_ref, k_ref, v_ref, g_ref, cos_ref, sin_ref, dm_ref, qd_ref, kd_ref, cd_ref, o_ref, s_ref = refs
    half = HEAD_W // 2
    n_chunks = seq // chunk
    k_scale = HEAD_W ** -0.5
    dmask = dm_ref[...]
    qdec = qd_ref[...]
    kdec = kd_ref[...]
    cdec = cd_ref[...]

    def rotate(x, cos, sin):
        x1, x2 = x[:, :half], x[:, half:]
        return jnp.concatenate([x1 * cos - x2 * sin, x2 * cos + x1 * sin], axis=1)

    def chunk_out(rows, pos, state):
        cos = cos_ref[pos, :]
        sin = sin_ref[pos, :]
        qc = rotate(q_ref[rows, :], cos, sin).astype(mxu_dtype)
        kf = rotate(k_ref[rows, :], cos, sin) * k_scale
        vc = v_ref[rows, :].astype(mxu_dtype)
        scores = lax.dot_general(qc, kf.astype(mxu_dtype), NT_DIMS, preferred_element_type=F32) * dmask
        o = (jnp.dot(scores.astype(mxu_dtype), vc, preferred_element_type=F32)
             + jnp.dot(qc, state.astype(mxu_dtype), preferred_element_type=F32) * qdec)
        new_state = state * cdec + lax.dot_general((kf * kdec).astype(mxu_dtype), vc, TN_DIMS,
                                                   preferred_element_type=F32)
        o = o * lax.rsqrt(jnp.mean(o * o, axis=-1, keepdims=True) + EPS)
        g = g_ref[rows, :]
        return o * (g * _sigmoid(g)), new_state

    for b in range(nb):
        state0 = s0_ref[b] if has_state else jnp.zeros((HEAD_W, HEAD_W), F32)
        if n_chunks == 1:
            out, state = chunk_out(pl.ds(b * seq, chunk), pl.ds(0, chunk), state0)
            o_ref[pl.ds(b * seq, chunk), :] = out.astype(o_ref.dtype)
        else:
            def body(c, state, b=b):
                start = pl.multiple_of(c * chunk, chunk)
                out, state = chunk_out(pl.ds(b * seq + start, chunk), pl.ds(start, chunk), state)
                o_ref[pl.ds(b * seq + start, chunk), :] = out.astype(o_ref.dtype)
                return state
            state = lax.fori_loop(0, n_chunks, body, state0, unroll=2 if n_chunks % 2 == 0 else 1)
        s_ref[b] = state


def _retention(proj, state0, l, tables, bsz, seq, heads, col_q, nb, out_dtype, mxu_dtype):
    cos, sin, dmask, qdec, kdec, cdec = tables
    m = proj.shape[0]
    chunk = dmask.shape[1]
    cb = col_q // HEAD_W
    rows = nb * seq

    def col(which):
        return pl.BlockSpec((rows, HEAD_W), lambda g, h: (g, cb + which * heads + h))

    in_specs = [
        col(0), col(1), col(2), col(3),
        pl.BlockSpec((seq, HEAD_W // 2), lambda g, h: (0, 0)),
        pl.BlockSpec((seq, HEAD_W // 2), lambda g, h: (0, 0)),
        pl.BlockSpec((None, chunk, chunk), lambda g, h: (h, 0, 0)),
        pl.BlockSpec((None, chunk, 1), lambda g, h: (h, 0, 0)),
        pl.BlockSpec((None, chunk, 1), lambda g, h: (h, 0, 0)),
        pl.BlockSpec((None, 1, 1), lambda g, h: (h, 0, 0)),
    ]
    args = [proj, proj, proj, proj, cos, sin, dmask, qdec, kdec, cdec]
    has_state = state0 is not None
    if has_state:
        in_specs.append(pl.BlockSpec((None, nb, None, HEAD_W, HEAD_W), lambda g, h: (l, g, h, 0, 0)))
        args.append(state0)
    return pl.pallas_call(
        functools.partial(_retention_kernel, nb=nb, seq=seq, chunk=chunk, has_state=has_state, mxu_dtype=mxu_dtype),
        grid=(bsz // nb, heads),
        in_specs=in_specs,
        out_specs=[
            pl.BlockSpec((rows, HEAD_W), lambda g, h: (g, h)),
            pl.BlockSpec((nb, None, HEAD_W, HEAD_W), lambda g, h: (g, h, 0, 0)),
        ],
        out_shape=[
            jax.ShapeDtypeStruct((m, heads * HEAD_W), out_dtype),
            jax.ShapeDtypeStruct((bsz, heads, HEAD_W, HEAD_W), F32),
        ],
        compiler_params=_params("parallel", "parallel"),
        name="retention",
    )(*args)


def _retention_tables(seq, first_pos, heads):
    half = HEAD_W // 2
    inv = 1.0 / (ROPE_BASE ** jnp.linspace(0.0, 1.0, half, dtype=F32))
    pos = (first_pos + jnp.arange(seq, dtype=jnp.int32)).astype(F32)
    ang = pos[:, None] * inv[None, :]
    chunk = seq if seq <= RET_CHUNK else RET_CHUNK
    lg = jnp.log(1.0 - 2.0 ** (-5.0 - jnp.arange(heads, dtype=F32)))
    idx = jnp.arange(chunk, dtype=F32)
    rel = idx[:, None] - idx[None, :]
    causal = rel >= 0
    dmask = jnp.where(causal[None], jnp.exp(jnp.where(causal, rel, 0.0)[None] * lg[:, None, None]), 0.0)
    qdec = jnp.exp((idx + 1.0)[None, :] * lg[:, None])[:, :, None]
    kdec = jnp.exp((chunk - 1.0 - idx)[None, :] * lg[:, None])[:, :, None]
    cdec = jnp.exp(chunk * lg)[:, None, None]
    return jnp.cos(ang), jnp.sin(ang), dmask, qdec, kdec, cdec


def kernel(x_prompt, x_sample, cache_k, cache_v, state_ret, page_table, c_prompt, c_sample, w_ada, b_ada, norm1, norm2, w_in, lambda_q1, lambda_k1, lambda_q2, lambda_k2, da_subln, w_pa, w_pr, w_o, w_up, w_down, norm_f):
    bp, seq, d = x_prompt.shape
    bs, n_new, _ = x_sample.shape
    depth = w_in.shape[0]
    page, heads = cache_k.shape[2], cache_k.shape[3]
    r_heads = state_ret.shape[2]
    assert cache_k.shape[4] == cache_v.shape[4] == state_ret.shape[3] == state_ret.shape[4] == HEAD_W
    assert heads == r_heads
    past_len = page_table.shape[1] * page
    d_ff = w_up.shape[2]
    hw = heads * HEAD_W
    col_k, col_v, col_qr = hw, 2 * hw, 3 * hw
    col_ga = 7 * hw

    c_rows = -(-(bp + bs) // 8) * 8
    c_all = jnp.concatenate([c_prompt, c_sample, jnp.zeros((c_rows - bp - bs, d), F32)], axis=0)
    mod = _ada(c_all, w_ada, b_ada)
    mod_p = mod[:, :bp].reshape(depth, bp, N_MOD, 1, d).transpose(0, 2, 1, 3, 4)
    mod_s = mod[:, bp:bp + bs].reshape(depth, bs, N_MOD, d).transpose(0, 2, 1, 3)
    mod_s = jnp.repeat(mod_s, n_new, axis=2)[:, :, None]

    lams = [a.reshape(depth, 1, -1) for a in (lambda_q1, lambda_k1, lambda_q2, lambda_k2)]
    subln = da_subln.reshape(depth, 1, HEAD_W)
    tab_p = _retention_tables(seq, 0, heads)
    tab_s = _retention_tables(n_new, past_len, heads)

    def dense_tail(x, mods, l, proj, gate_col, o_a, o_r, decode):
        wide, square, deep = (TILES_DECODE,) * 3 if decode else (TILES_PROMPT_WIDE, TILES_PROMPT_SQUARE, TILES_PROMPT_DEEP)
        mixed = _mix(o_a, o_r, w_pa, w_pr, proj, l, gate_col, gate_col + d, d, square)
        x = _matmul(mixed, w_o, l, d, F32, square, "resid", x, mods, 2)
        h2 = _norm_mod(x, norm2, mods, l, 4, 3)
        u = _matmul(h2, w_up, l, d_ff, BF16, wide, "relu2")
        return _matmul(u, w_down, l, d, F32, deep, "resid", x, mods, 5)

    xp = x_prompt.reshape(bp * seq, d)
    xs = x_sample.reshape(bs * n_new, d)
    kp, vp, sp, ksm, vsm, ssm = [], [], [], [], [], []
    for l in range(depth):
        lam_init = 0.8 - 0.6 * math.exp(-0.3 * l)

        h = _norm_mod(xp, norm1, mod_p, l, 1, 0)
        q, k, v = (_matmul(h, w_in, l, hw, F32, TILES_PROMPT_WIDE, col0=c) for c in (0, col_k, col_v))
        rest = _matmul(h, w_in, l, w_in.shape[2] - col_qr, F32, TILES_PROMPT_WIDE, col0=col_qr)
        o_a = _attn_prompt(q, k, v, lams, subln, l, lam_init, bp, seq, heads)
        o_r, s_new = _retention(rest, None, l, tab_p, bp, seq, heads, 0, 1, BF16, BF16)
        kp.append(k.reshape(bp, seq, heads, HEAD_W))
        vp.append(v.reshape(bp, seq, heads, HEAD_W))
        sp.append(s_new)
        xp = dense_tail(xp, mod_p, l, rest, col_ga - col_qr, o_a, o_r, False)

        h = _norm_mod(xs, norm1, mod_s, l, 1, 0)
        proj = _matmul(h, w_in, l, w_in.shape[2], F32, TILES_DECODE)
        k_new = proj[:, col_k:col_k + hw].reshape(bs, n_new, heads, HEAD_W)
        v_new = proj[:, col_v:col_v + hw].reshape(bs, n_new, heads, HEAD_W)
        o_a = _attn_decode(proj, k_new, v_new, cache_k, cache_v, page_table, lams, subln, l, lam_init,
                           bs, n_new, heads)
        o_r, s_new = _retention(proj, state_ret, l, tab_s, bs, n_new, heads, col_qr, bs, F32, F32)
        ksm.append(k_new)
        vsm.append(v_new)
        ssm.append(s_new)
        xs = dense_tail(xs, mod_s, l, proj, col_ga, o_a, o_r.astype(BF16), True)

    y_prompt = _final_norm(xp, norm_f).reshape(bp, seq, d)
    y_sample = _final_norm(xs, norm_f).reshape(bs, n_new, d)
    return (y_prompt, y_sample, jnp.stack(kp), jnp.stack(vp), jnp.stack(sp),
            jnp.stack(ksm), jnp.stack(vsm), jnp.stack(ssm))
```

```python
import functools
import math

import jax
import jax.numpy as jnp
from jax import lax
from jax.experimental import pallas as pl
from jax.experimental.pallas import tpu as pltpu

F32 = jnp.float32
BF16 = jnp.bfloat16

EPS = 1e-6
ROPE_BASE = 10000.0
RET_CHUNK = 128
N_MOD = 6
HEAD_W = 256
VMEM_LIMIT_BYTES = 60 * 1024 * 1024

LARGE_BLOCK_BYTES = 12 * 1024 * 1024
ROW_BLOCK_MAX = 2304
TILES_PROMPT_WIDE = (2048, 512, 4096)
TILES_PROMPT_SQUARE = (1024, 512, 4096)
TILES_PROMPT_DEEP = (2048, 1024, 1024)
TILES_DECODE = (64, 1024, 4096)
MXU_COLS = 256
PAGES_PER_STEP = 4

NT_DIMS = (((1,), (1,)), ((), ()))
TN_DIMS = (((0,), (0,)), ((), ()))


def _params(*sem):
    return pltpu.CompilerParams(dimension_semantics=sem, vmem_limit_bytes=VMEM_LIMIT_BYTES)


def _sigmoid(x):
    return 1.0 / (1.0 + jnp.exp(-x))


def _tile(n, want):
    t = min(n, want)
    while n % t:
        t -= 1
    return t


def _tile16(n, want):
    return max(t for t in range(16, min(n, want) + 1, 16) if n % t == 0)


def _ada_kernel(c_ref, w_ref, b_ref, o_ref):
    c = c_ref[...]
    a = (c * _sigmoid(c)).astype(BF16)
    o_ref[...] = jnp.dot(a, w_ref[...].astype(BF16), preferred_element_type=F32) + b_ref[...]


def _ada(c_all, w_ada, b_ada):
    depth, d, n = w_ada.shape
    rows = c_all.shape[0]
    tn = _tile(n, 512)
    return pl.pallas_call(
        _ada_kernel,
        grid=(depth, n // tn),
        in_specs=[
            pl.BlockSpec((rows, d), lambda l, j: (0, 0)),
            pl.BlockSpec((None, d, tn), lambda l, j: (l, 0, j)),
            pl.BlockSpec((None, 1, tn), lambda l, j: (l, 0, j)),
        ],
        out_specs=pl.BlockSpec((None, rows, tn), lambda l, j: (l, 0, j)),
        out_shape=jax.ShapeDtypeStruct((depth, rows, n), F32),
        compiler_params=_params("parallel", "parallel"),
        name="ada",
    )(c_all, w_ada, b_ada.reshape(depth, 1, n))


def _norm_mod_kernel(x_ref, g_ref, sc_ref, sh_ref, o_ref):
    x = x_ref[...]
    y = x * lax.rsqrt(jnp.mean(x * x, axis=-1, keepdims=True) + EPS) * g_ref[...]
    o_ref[...] = (y * (1.0 + sc_ref[...]) + sh_ref[...]).astype(o_ref.dtype)


def _norm_mod_into_kernel(x_ref, g_ref, sc_ref, sh_ref, dst_ref, o_ref):
    del dst_ref
    _norm_mod_kernel(x_ref, g_ref, sc_ref, sh_ref, o_ref)


def _norm_mod(x, gain, mods, l, m_scale, m_shift, total_rows, dst=None):
    m, d = x.shape
    groups, r = mods.shape[2], mods.shape[3]
    tm = m // groups if r > 1 else _tile(m // groups, 512)
    per_group = (m // groups) // tm
    first = 0 if dst is None else (total_rows - m) // tm
    assert dst is None or (total_rows - m) % tm == 0
    in_specs = [
        pl.BlockSpec((tm, d), lambda i: (i, 0)),
        pl.BlockSpec((None, 1, d), lambda i: (l, 0, 0)),
        pl.BlockSpec((None, None, None, r, d), lambda i: (l, m_scale, i // per_group, 0, 0)),
        pl.BlockSpec((None, None, None, r, d), lambda i: (l, m_shift, i // per_group, 0, 0)),
    ]
    args = [x, gain.reshape(gain.shape[0], 1, d), mods, mods]
    if dst is not None:
        in_specs.append(pl.BlockSpec(memory_space=pl.ANY))
        args.append(dst)
    return pl.pallas_call(
        _norm_mod_kernel if dst is None else _norm_mod_into_kernel,
        grid=(m // tm,),
        in_specs=in_specs,
        out_specs=pl.BlockSpec((tm, d), lambda i: (first + i, 0)),
        out_shape=jax.ShapeDtypeStruct((total_rows, d), BF16),
        input_output_aliases={} if dst is None else {4: 0},
        compiler_params=_params("parallel"),
        name="norm_mod",
    )(*args)


def _norm_kernel(x_ref, g_ref, o_ref):
    x = x_ref[...]
    o_ref[...] = x * lax.rsqrt(jnp.mean(x * x, axis=-1, keepdims=True) + EPS) * g_ref[...]


def _final_norm(x, gain):
    m, d = x.shape
    tm = _tile(m, 256)
    return pl.pallas_call(
        _norm_kernel,
        grid=(m // tm,),
        in_specs=[pl.BlockSpec((tm, d), lambda i: (i, 0)), pl.BlockSpec((1, d), lambda i: (0, 0))],
        out_specs=pl.BlockSpec((tm, d), lambda i: (i, 0)),
        out_shape=jax.ShapeDtypeStruct((m, d), F32),
        compiler_params=_params("parallel"),
        name="final_norm",
    )(x, gain.reshape(1, d))


def _mm_kernel(*refs, nk, epilogue):
    if epilogue == "resid":
        a_ref, w_ref, x_ref, g_ref, o_ref = refs
    else:
        a_ref, w_ref, o_ref = refs
    tn = o_ref.shape[1]
    sub = _tile(tn, MXU_COLS) if nk > 1 else tn

    def run(first, last):
        for c in range(tn // sub):
            cols = slice(c * sub, (c + 1) * sub)
            acc = jnp.dot(a_ref[...], w_ref[:, cols].astype(BF16), preferred_element_type=F32)
            if not first:
                acc = o_ref[:, cols] + acc
            if not last:
                o_ref[:, cols] = acc
            elif epilogue == "relu2":
                r = jnp.maximum(acc, 0.0)
                o_ref[:, cols] = (r * r).astype(o_ref.dtype)
            elif epilogue == "resid":
                o_ref[:, cols] = x_ref[:, cols] + g_ref[:, cols] * acc
            else:
                o_ref[:, cols] = acc.astype(o_ref.dtype)

    if nk == 1:
        run(True, True)
        return
    k = pl.program_id(2)
    pl.when(k == 0)(lambda: run(True, False))
    pl.when(jnp.logical_and(k > 0, k < nk - 1))(lambda: run(False, False))
    pl.when(k == nk - 1)(lambda: run(False, True))


def _matmul(a, w, l, n, out_dtype, tiles, epilogue="none", resid=None, mods=None, m_gate=None, rows=None):
    row0, m = rows if rows is not None else (0, a.shape[0])
    kdim = a.shape[1]
    groups = mods.shape[2] if epilogue == "resid" else 1
    tm = _tile(m // groups, tiles[0])
    tn = _tile(n, tiles[1])
    tk = _tile(kdim, tiles[2])
    nk = kdim // tk
    assert nk == 1 or out_dtype == F32
    assert row0 % tm == 0
    ib = row0 // tm
    once = pl.Buffered(1)
    big = tm * tk * a.dtype.itemsize > LARGE_BLOCK_BYTES
    in_specs = [
        pl.BlockSpec((tm, tk), lambda i, j, k: (ib + i, k), pipeline_mode=once if nk == 1 and big else None),
        pl.BlockSpec((None, tk, tn), lambda i, j, k: (l, k, j)),
    ]
    args = [a, w]
    if epilogue == "resid":
        r = mods.shape[3]
        assert r == 1 or (groups == 1 and r == tm == m)
        per_group = (m // groups) // tm
        in_specs += [
            pl.BlockSpec((tm, tn), lambda i, j, k: (i, j), pipeline_mode=once if nk > 1 else None),
            pl.BlockSpec((None, None, None, r, tn), lambda i, j, k: (l, m_gate, i // per_group, 0, j)),
        ]
        args += [resid, mods]
    return pl.pallas_call(
        functools.partial(_mm_kernel, nk=nk, epilogue=epilogue),
        grid=(m // tm, n // tn, nk),
        in_specs=in_specs,
        out_specs=pl.BlockSpec((tm, tn), lambda i, j, k: (i, j)),
        out_shape=jax.ShapeDtypeStruct((m, n), out_dtype),
        compiler_params=_params("parallel", "parallel", "arbitrary"),
        name="mm_" + epilogue,
    )(*args)


def _mix_kernel(oa_ref, or_ref, wpa_ref, wpr_ref, ga_ref, gr_ref, o_ref):
    pa = jnp.dot(oa_ref[...], wpa_ref[...].astype(BF16), preferred_element_type=F32)
    pr = jnp.dot(or_ref[...], wpr_ref[...].astype(BF16), preferred_element_type=F32)
    o_ref[...] = (_sigmoid(ga_ref[...]) * pa + _sigmoid(gr_ref[...]) * pr).astype(o_ref.dtype)


def _mix(o_a, o_r, w_pa, w_pr, proj, row0, l, col_ga, col_gr, d, tiles):
    m, ka = o_a.shape
    kr = o_r.shape[1]
    tm = _tile(m, tiles[0])
    tn = _tile(math.gcd(d, col_ga, col_gr), tiles[1])
    assert row0 % tm == 0
    ib = row0 // tm
    return pl.pallas_call(
        _mix_kernel,
        grid=(m // tm, d // tn),
        in_specs=[
            pl.BlockSpec((tm, ka), lambda i, j: (i, 0)),
            pl.BlockSpec((tm, kr), lambda i, j: (i, 0)),
            pl.BlockSpec((None, ka, tn), lambda i, j: (l, 0, j)),
            pl.BlockSpec((None, kr, tn), lambda i, j: (l, 0, j)),
            pl.BlockSpec((tm, tn), lambda i, j: (ib + i, col_ga // tn + j)),
            pl.BlockSpec((tm, tn), lambda i, j: (ib + i, col_gr // tn + j)),
        ],
        out_specs=pl.BlockSpec((tm, tn), lambda i, j: (i, j)),
        out_shape=jax.ShapeDtypeStruct((m, d), BF16),
        compiler_params=_params("parallel", "parallel"),
        name="mix",
    )(o_a, o_r, w_pa, w_pr, proj, proj)


def _lambda(lq1, lk1, lq2, lk2, lam_init):
    return (jnp.exp(jnp.sum(lq1[...] * lk1[...], axis=-1, keepdims=True))
            - jnp.exp(jnp.sum(lq2[...] * lk2[...], axis=-1, keepdims=True)) + lam_init)


def _subln(o, gain, lam_init):
    y = o * lax.rsqrt(jnp.mean(o * o, axis=-1, keepdims=True) + EPS) * gain
    return y * (1.0 - lam_init)


def _attn_prompt_kernel(q_ref, k_ref, v_ref, lq1, lk1, lq2, lk2, g_ref, o_ref, ko_ref, vo_ref,
                        kb, vb, s_ref, acc_ref, *, lam_init, t, tk, scale):
    qi = pl.program_id(2)
    half = HEAD_W // 2

    @pl.when(qi == 0)
    def _():
        k = k_ref[...]
        v = v_ref[...]
        ko_ref[...] = k
        vo_ref[...] = v
        kb[...] = k.astype(BF16)
        vb[...] = v.astype(BF16)

    lam = _lambda(lq1, lk1, lq2, lk2, lam_init)
    q = q_ref[...].astype(BF16)
    q1, q2 = q[:, :half], q[:, half:]

    def scores(j):
        kj = kb[pl.ds(pl.multiple_of(j * tk, tk), tk), :]
        s1 = lax.dot_general(q1, kj[:, :half], NT_DIMS, preferred_element_type=F32) * scale
        s2 = lax.dot_general(q2, kj[:, half:], NT_DIMS, preferred_element_type=F32) * scale
        return s1, s2

    def score_body(j, carry):
        m1, m2 = carry
        s1, s2 = scores(j)
        s_ref[0, j] = s1
        s_ref[1, j] = s2
        return (jnp.maximum(m1, jnp.max(s1, axis=-1, keepdims=True)),
                jnp.maximum(m2, jnp.max(s2, axis=-1, keepdims=True)))

    n_full = (qi * t) // tk
    neg = jnp.full((t, 1), -jnp.inf, F32)
    m1, m2 = lax.fori_loop(0, n_full, score_body, (neg, neg))
    s1, s2 = scores(n_full)
    q_pos = qi * t + lax.broadcasted_iota(jnp.int32, (t, tk), 0)
    k_pos = n_full * tk + lax.broadcasted_iota(jnp.int32, (t, tk), 1)
    keep = k_pos <= q_pos
    s1 = jnp.where(keep, s1, -jnp.inf)
    s2 = jnp.where(keep, s2, -jnp.inf)
    s_ref[0, n_full] = s1
    s_ref[1, n_full] = s2
    m1 = jnp.maximum(m1, jnp.max(s1, axis=-1, keepdims=True))
    m2 = jnp.maximum(m2, jnp.max(s2, axis=-1, keepdims=True))

    acc_ref[...] = jnp.zeros(acc_ref.shape, F32)

    def pv_body(j, carry):
        l1, l2 = carry
        e1 = jnp.exp(s_ref[0, j] - m1)
        e2 = jnp.exp(s_ref[1, j] - m2)
        vj = vb[pl.ds(pl.multiple_of(j * tk, tk), tk), :]
        acc_ref[0] += jnp.dot(e1.astype(BF16), vj, preferred_element_type=F32)
        acc_ref[1] += jnp.dot(e2.astype(BF16), vj, preferred_element_type=F32)
        return l1 + jnp.sum(e1, axis=-1, keepdims=True), l2 + jnp.sum(e2, axis=-1, keepdims=True)

    zero = jnp.zeros((t, 1), F32)
    l1, l2 = lax.fori_loop(0, n_full + 1, pv_body, (zero, zero))
    o = acc_ref[0] * (1.0 / l1) - acc_ref[1] * (lam / l2)
    o_ref[...] = _subln(o, g_ref[...], lam_init).astype(o_ref.dtype)


def _attn_prompt(proj, lams, subln, l, lam_init, bsz, seq, heads):
    m = bsz * seq
    tk = _tile(seq, 512)
    t = tk
    nq = seq // t
    lam_spec = pl.BlockSpec((None, 1, HEAD_W // 2), lambda b, h, i: (l, 0, 0))
    kv_spec = pl.BlockSpec((seq, HEAD_W), lambda b, h, i: (b, h))
    return pl.pallas_call(
        functools.partial(_attn_prompt_kernel, lam_init=lam_init, t=t, tk=tk, scale=(HEAD_W // 2) ** -0.5),
        grid=(bsz, heads, nq),
        in_specs=[
            pl.BlockSpec((t, HEAD_W), lambda b, h, i: (b * nq + i, h)),
            pl.BlockSpec((seq, HEAD_W), lambda b, h, i: (b, heads + h)),
            pl.BlockSpec((seq, HEAD_W), lambda b, h, i: (b, 2 * heads + h)),
            lam_spec, lam_spec, lam_spec, lam_spec,
            pl.BlockSpec((None, 1, HEAD_W), lambda b, h, i: (l, 0, 0)),
        ],
        out_specs=[pl.BlockSpec((t, HEAD_W), lambda b, h, i: (b * nq + i, h)), kv_spec, kv_spec],
        out_shape=[
            jax.ShapeDtypeStruct((m, heads * HEAD_W), BF16),
            jax.ShapeDtypeStruct((m, heads * HEAD_W), F32),
            jax.ShapeDtypeStruct((m, heads * HEAD_W), F32),
        ],
        scratch_shapes=[
            pltpu.VMEM((seq, HEAD_W), BF16),
            pltpu.VMEM((seq, HEAD_W), BF16),
            pltpu.VMEM((2, seq // tk, t, tk), F32),
            pltpu.VMEM((2, t, HEAD_W), F32),
        ],
        compiler_params=_params("parallel", "parallel", "arbitrary"),
        name="attn_prompt",
    )(proj, proj, proj, *lams, subln)


def _attn_decode_kernel(pt_ref, q_ref, kn_ref, vn_ref, *rest, lam_init, heads, n_new, n_pp, scale):
    kp_refs, vp_refs = rest[:n_pp], rest[n_pp:2 * n_pp]
    lq1, lk1, lq2, lk2, g_ref, o_ref, qall, bias_ref, m_ref, l_ref, acc_ref = rest[2 * n_pp:]
    p = pl.program_id(1)
    half = HEAD_W // 2
    grp = 2 * n_new
    rows = heads * grp
    page_rows = bias_ref.shape[1]

    def head_of_row(shape):
        return lax.broadcasted_iota(jnp.int32, shape, 0) // grp

    @pl.when(p == 0)
    def _():
        q = q_ref[...]
        z = jnp.zeros((n_new, half), F32)
        blocks = []
        for h in range(heads):
            q1 = q[:, h * HEAD_W:h * HEAD_W + half]
            q2 = q[:, h * HEAD_W + half:(h + 1) * HEAD_W]
            blocks.append(jnp.concatenate([q1, z], axis=1))
            blocks.append(jnp.concatenate([z, q2], axis=1))
        qall[...] = jnp.concatenate(blocks, axis=0).astype(BF16)
        shape = (rows, page_rows)
        same_head = lax.broadcasted_iota(jnp.int32, shape, 1) % heads == head_of_row(shape)
        bias_ref[...] = jnp.where(same_head, 0.0, -jnp.inf)
        m_ref[...] = jnp.full(m_ref.shape, -jnp.inf, F32)
        l_ref[...] = jnp.zeros(l_ref.shape, F32)
        acc_ref[...] = jnp.zeros(acc_ref.shape, F32)

    def absorb(keys, values, biases):
        qa = qall[...]
        ss = [lax.dot_general(qa, k.astype(BF16), NT_DIMS, preferred_element_type=F32) * scale + b
              for k, b in zip(keys, biases)]
        m_old = m_ref[...]
        m_new = m_old
        for s in ss:
            m_new = jnp.maximum(m_new, jnp.max(s, axis=-1, keepdims=True))
        alpha = jnp.exp(m_old - m_new)
        lsum = alpha * l_ref[...]
        acc = alpha * acc_ref[...]
        for s, v in zip(ss, values):
            e = jnp.exp(s - m_new)
            lsum = lsum + jnp.sum(e, axis=-1, keepdims=True)
            acc = acc + jnp.dot(e.astype(BF16), v.astype(BF16), preferred_element_type=F32)
        m_ref[...] = m_new
        l_ref[...] = lsum
        acc_ref[...] = acc

    bias = bias_ref[...]
    absorb([r[...] for r in kp_refs], [r[...] for r in vp_refs], [bias] * n_pp)

    @pl.when(p == pl.num_programs(1) - 1)
    def _():
        shape = (rows, n_new * heads)
        col = lax.broadcasted_iota(jnp.int32, shape, 1)
        tok = lax.broadcasted_iota(jnp.int32, shape, 0) % n_new
        ok = jnp.logical_and(col % heads == head_of_row(shape), col // heads <= tok)
        absorb([kn_ref[...]], [vn_ref[...]], [jnp.where(ok, 0.0, -jnp.inf)])
        lam = _lambda(lq1, lk1, lq2, lk2, lam_init)
        acc = acc_ref[...]
        lsum = l_ref[...]
        outs = []
        for h in range(heads):
            r0 = h * grp
            o = (acc[r0:r0 + n_new] / lsum[r0:r0 + n_new]
                 - lam * (acc[r0 + n_new:r0 + grp] / lsum[r0 + n_new:r0 + grp]))
            outs.append(_subln(o, g_ref[...], lam_init))
        o_ref[...] = jnp.concatenate(outs, axis=1).astype(o_ref.dtype)


def _attn_decode(proj, row0, k_new, v_new, cache_k, cache_v, page_table, lams, subln, l, lam_init, bsz, n_new, heads):
    depth, n_pool, page = cache_k.shape[:3]
    assert row0 % n_new == 0
    qb = row0 // n_new
    width = heads * HEAD_W
    n_pages = page_table.shape[1]
    n_pp = _tile(n_pages, PAGES_PER_STEP)
    page_rows = page * heads
    ck = cache_k.reshape(depth, n_pool, page_rows, HEAD_W)
    cv = cache_v.reshape(depth, n_pool, page_rows, HEAD_W)
    kn = k_new.reshape(bsz, n_new * heads, HEAD_W)
    vn = v_new.reshape(bsz, n_new * heads, HEAD_W)
    lam_spec = pl.BlockSpec((None, 1, HEAD_W // 2), lambda b, p, pt: (l, 0, 0))
    page_specs = [pl.BlockSpec((None, None, page_rows, HEAD_W), lambda b, p, pt, i=i: (l, pt[b, p * n_pp + i], 0, 0))
                  for i in range(n_pp)]
    new_spec = pl.BlockSpec((None, n_new * heads, HEAD_W), lambda b, p, pt: (b, 0, 0))
    rows = heads * 2 * n_new
    grid_spec = pltpu.PrefetchScalarGridSpec(
        num_scalar_prefetch=1,
        grid=(bsz, n_pages // n_pp),
        in_specs=[pl.BlockSpec((n_new, width), lambda b, p, pt: (qb + b, 0)), new_spec, new_spec]
        + page_specs + page_specs
        + [lam_spec, lam_spec, lam_spec, lam_spec, pl.BlockSpec((None, 1, HEAD_W), lambda b, p, pt: (l, 0, 0))],
        out_specs=pl.BlockSpec((None, n_new, width), lambda b, p, pt: (b, 0, 0)),
        scratch_shapes=[
            pltpu.VMEM((rows, HEAD_W), BF16),
            pltpu.VMEM((rows, page_rows), F32),
            pltpu.VMEM((rows, 1), F32),
            pltpu.VMEM((rows, 1), F32),
            pltpu.VMEM((rows, HEAD_W), F32),
        ],
    )
    out = pl.pallas_call(
        functools.partial(_attn_decode_kernel, lam_init=lam_init, heads=heads, n_new=n_new, n_pp=n_pp,
                          scale=(HEAD_W // 2) ** -0.5),
        grid_spec=grid_spec,
        out_shape=jax.ShapeDtypeStruct((bsz, n_new, width), BF16),
        compiler_params=_params("parallel", "arbitrary"),
        name="attn_decode",
    )(page_table, proj, kn, vn, *([ck] * n_pp), *([cv] * n_pp), *lams, subln)
    return out.reshape(bsz * n_new, width)


def _retention_kernel(*refs, nb, seq, chunk, has_state, mxu_dtype):
    if has_state:
        q_ref, k_ref, v_ref, g_ref, cos_ref, sin_ref, dm_ref, qd_ref, kd_ref, cd_ref, s0_ref, o_ref, s_ref = refs
    else:
        q_ref, k_ref, v_ref, g_ref, cos_ref, sin_ref, dm_ref, qd_ref, kd_ref, cd_ref, o_ref, s_ref = refs
    half = HEAD_W // 2
    n_chunks = seq // chunk
    k_scale = HEAD_W ** -0.5
    dmask = dm_ref[...]
    qdec = qd_ref[...]
    kdec = kd_ref[...]
    cdec = cd_ref[...]

    def rotate(x, cos, sin):
        x1, x2 = x[:, :half], x[:, half:]
        return jnp.concatenate([x1 * cos - x2 * sin, x2 * cos + x1 * sin], axis=1)

    def chunk_out(rows, pos, state):
        cos = cos_ref[pos, :]
        sin = sin_ref[pos, :]
        qc = rotate(q_ref[rows, :], cos, sin).astype(mxu_dtype)
        kf = rotate(k_ref[rows, :], cos, sin) * k_scale
        vc = v_ref[rows, :].astype(mxu_dtype)
        scores = lax.dot_general(qc, kf.astype(mxu_dtype), NT_DIMS, preferred_element_type=F32) * dmask
        o = (jnp.dot(scores.astype(mxu_dtype), vc, preferred_element_type=F32)
             + jnp.dot(qc, state.astype(mxu_dtype), preferred_element_type=F32) * qdec)
        new_state = state * cdec + lax.dot_general((kf * kdec).astype(mxu_dtype), vc, TN_DIMS,
                                                   preferred_element_type=F32)
        o = o * lax.rsqrt(jnp.mean(o * o, axis=-1, keepdims=True) + EPS)
        g = g_ref[rows, :]
        return o * (g * _sigmoid(g)), new_state

    for b in range(nb):
        state0 = s0_ref[b] if has_state else jnp.zeros((HEAD_W, HEAD_W), F32)
        if n_chunks == 1:
            out, state = chunk_out(pl.ds(b * seq, chunk), pl.ds(0, chunk), state0)
            o_ref[pl.ds(b * seq, chunk), :] = out.astype(o_ref.dtype)
        else:
            def body(c, state, b=b):
                start = pl.multiple_of(c * chunk, chunk)
                out, state = chunk_out(pl.ds(b * seq + start, chunk), pl.ds(start, chunk), state)
                o_ref[pl.ds(b * seq + start, chunk), :] = out.astype(o_ref.dtype)
                return state
            state = lax.fori_loop(0, n_chunks, body, state0, unroll=2 if n_chunks % 2 == 0 else 1)
        s_ref[b] = state


def _retention(proj, row0, state0, l, tables, bsz, seq, heads, col_q, nb, out_dtype, mxu_dtype):
    cos, sin, dmask, qdec, kdec, cdec = tables
    m = bsz * seq
    chunk = dmask.shape[1]
    cb = col_q // HEAD_W
    rows = nb * seq
    assert row0 % rows == 0
    rb = row0 // rows

    def col(which):
        return pl.BlockSpec((rows, HEAD_W), lambda g, h: (rb + g, cb + which * heads + h))

    in_specs = [
        col(0), col(1), col(2), col(3),
        pl.BlockSpec((seq, HEAD_W // 2), lambda g, h: (0, 0)),
        pl.BlockSpec((seq, HEAD_W // 2), lambda g, h: (0, 0)),
        pl.BlockSpec((None, chunk, chunk), lambda g, h: (h, 0, 0)),
        pl.BlockSpec((None, chunk, 1), lambda g, h: (h, 0, 0)),
        pl.BlockSpec((None, chunk, 1), lambda g, h: (h, 0, 0)),
        pl.BlockSpec((None, 1, 1), lambda g, h: (h, 0, 0)),
    ]
    args = [proj, proj, proj, proj, cos, sin, dmask, qdec, kdec, cdec]
    has_state = state0 is not None
    if has_state:
        in_specs.append(pl.BlockSpec((None, nb, None, HEAD_W, HEAD_W), lambda g, h: (l, g, h, 0, 0)))
        args.append(state0)
    return pl.pallas_call(
        functools.partial(_retention_kernel, nb=nb, seq=seq, chunk=chunk, has_state=has_state, mxu_dtype=mxu_dtype),
        grid=(bsz // nb, heads),
        in_specs=in_specs,
        out_specs=[
            pl.BlockSpec((rows, HEAD_W), lambda g, h: (g, h)),
            pl.BlockSpec((nb, None, HEAD_W, HEAD_W), lambda g, h: (g, h, 0, 0)),
        ],
        out_shape=[
            jax.ShapeDtypeStruct((m, heads * HEAD_W), out_dtype),
            jax.ShapeDtypeStruct((bsz, heads, HEAD_W, HEAD_W), F32),
        ],
        compiler_params=_params("parallel", "parallel"),
        name="retention",
    )(*args)


def _retention_tables(seq, first_pos, heads):
    half = HEAD_W // 2
    inv = 1.0 / (ROPE_BASE ** jnp.linspace(0.0, 1.0, half, dtype=F32))
    pos = (first_pos + jnp.arange(seq, dtype=jnp.int32)).astype(F32)
    ang = pos[:, None] * inv[None, :]
    chunk = seq if seq <= RET_CHUNK else RET_CHUNK
    lg = jnp.log(1.0 - 2.0 ** (-5.0 - jnp.arange(heads, dtype=F32)))
    idx = jnp.arange(chunk, dtype=F32)
    rel = idx[:, None] - idx[None, :]
    causal = rel >= 0
    dmask = jnp.where(causal[None], jnp.exp(jnp.where(causal, rel, 0.0)[None] * lg[:, None, None]), 0.0)
    qdec = jnp.exp((idx + 1.0)[None, :] * lg[:, None])[:, :, None]
    kdec = jnp.exp((chunk - 1.0 - idx)[None, :] * lg[:, None])[:, :, None]
    cdec = jnp.exp(chunk * lg)[:, None, None]
    return jnp.cos(ang), jnp.sin(ang), dmask, qdec, kdec, cdec


def kernel(x_prompt, x_sample, cache_k, cache_v, state_ret, page_table, c_prompt, c_sample, w_ada, b_ada, norm1, norm2, w_in, lambda_q1, lambda_k1, lambda_q2, lambda_k2, da_subln, w_pa, w_pr, w_o, w_up, w_down, norm_f):
    bp, seq, d = x_prompt.shape
    bs, n_new, _ = x_sample.shape
    depth = w_in.shape[0]
    page, heads = cache_k.shape[2], cache_k.shape[3]
    r_heads = state_ret.shape[2]
    assert cache_k.shape[4] == cache_v.shape[4] == state_ret.shape[3] == state_ret.shape[4] == HEAD_W
    assert heads == r_heads
    past_len = page_table.shape[1] * page
    d_ff = w_up.shape[2]
    hw = heads * HEAD_W
    col_k, col_v, col_qr = hw, 2 * hw, 3 * hw
    col_ga = 7 * hw

    c_rows = -(-(bp + bs) // 8) * 8
    c_all = jnp.concatenate([c_prompt, c_sample, jnp.zeros((c_rows - bp - bs, d), F32)], axis=0)
    mod = _ada(c_all, w_ada, b_ada)
    mod_p = mod[:, :bp].reshape(depth, bp, N_MOD, 1, d).transpose(0, 2, 1, 3, 4)
    mod_s = mod[:, bp:bp + bs].reshape(depth, bs, N_MOD, d).transpose(0, 2, 1, 3)
    mod_s = jnp.repeat(mod_s, n_new, axis=2)[:, :, None]

    lams = [a.reshape(depth, 1, -1) for a in (lambda_q1, lambda_k1, lambda_q2, lambda_k2)]
    subln = da_subln.reshape(depth, 1, HEAD_W)
    tab_p = _retention_tables(seq, 0, heads)
    tab_s = _retention_tables(n_new, past_len, heads)

    mp, ms = bp * seq, bs * n_new
    rows_all = mp + ms
    wide_all = (_tile16(rows_all, ROW_BLOCK_MAX),) + TILES_PROMPT_WIDE[1:]

    def norm_both(xp, xs, gain, l, m_scale, m_shift):
        h = _norm_mod(xp, gain, mod_p, l, m_scale, m_shift, rows_all)
        return _norm_mod(xs, gain, mod_s, l, m_scale, m_shift, rows_all, dst=h)

    xp = x_prompt.reshape(mp, d)
    xs = x_sample.reshape(ms, d)
    kp, vp, sp, ksm, vsm, ssm = [], [], [], [], [], []
    for l in range(depth):
        lam_init = 0.8 - 0.6 * math.exp(-0.3 * l)

        h = norm_both(xp, xs, norm1, l, 1, 0)
        proj = _matmul(h, w_in, l, w_in.shape[2], F32, wide_all)

        o_a_p, k_p, v_p = _attn_prompt(proj, lams, subln, l, lam_init, bp, seq, heads)
        o_r_p, s_p = _retention(proj, 0, None, l, tab_p, bp, seq, heads, col_qr, 1, BF16, BF16)
        kp.append(k_p.reshape(bp, seq, heads, HEAD_W))
        vp.append(v_p.reshape(bp, seq, heads, HEAD_W))
        sp.append(s_p)

        k_s = proj[mp:, col_k:col_k + hw].reshape(bs, n_new, heads, HEAD_W)
        v_s = proj[mp:, col_v:col_v + hw].reshape(bs, n_new, heads, HEAD_W)
        o_a_s = _attn_decode(proj, mp, k_s, v_s, cache_k, cache_v, page_table, lams, subln, l, lam_init,
                             bs, n_new, heads)
        o_r_s, s_s = _retention(proj, mp, state_ret, l, tab_s, bs, n_new, heads, col_qr, bs, F32, F32)
        ksm.append(k_s)
        vsm.append(v_s)
        ssm.append(s_s)

        mixed_p = _mix(o_a_p, o_r_p, w_pa, w_pr, proj, 0, l, col_ga, col_ga + d, d, TILES_PROMPT_SQUARE)
        mixed_s = _mix(o_a_s, o_r_s.astype(BF16), w_pa, w_pr, proj, mp, l, col_ga, col_ga + d, d, TILES_DECODE)
        xp = _matmul(mixed_p, w_o, l, d, F32, TILES_PROMPT_SQUARE, "resid", xp, mod_p, 2)
        xs = _matmul(mixed_s, w_o, l, d, F32, TILES_DECODE, "resid", xs, mod_s, 2)

        h2 = norm_both(xp, xs, norm2, l, 4, 3)
        u = _matmul(h2, w_up, l, d_ff, BF16, wide_all, "relu2")
        xp = _matmul(u, w_down, l, d, F32, TILES_PROMPT_DEEP, "resid", xp, mod_p, 5, rows=(0, mp))
        xs = _matmul(u, w_down, l, d, F32, TILES_DECODE, "resid", xs, mod_s, 5, rows=(mp, ms))

    y_prompt = _final_norm(xp, norm_f).reshape(bp, seq, d)
    y_sample = _final_norm(xs, norm_f).reshape(bs, n_new, d)
    return (y_prompt, y_sample, jnp.stack(kp), jnp.stack(vp), jnp.stack(sp),
            jnp.stack(ksm), jnp.stack(vsm), jnp.stack(ssm))
```

```python
import functools
import math

import jax
import jax.numpy as jnp
from jax import lax
from jax.experimental import pallas as pl
from jax.experimental.pallas import tpu as pltpu

F32 = jnp.float32
BF16 = jnp.bfloat16

EPS = 1e-6
ROPE_BASE = 10000.0
RET_CHUNK = 128
N_MOD = 6
HEAD_W = 256
VMEM_LIMIT_BYTES = 60 * 1024 * 1024

LARGE_BLOCK_BYTES = 12 * 1024 * 1024
ROW_BLOCK_MAX = 2304
TILES_PROMPT_WIDE = (2048, 512, 4096)
TILES_PROMPT_SQUARE = (1024, 512, 4096)
TILES_PROMPT_DEEP = (2048, 1024, 1024)
TILES_DECODE = (64, 1024, 4096)
MXU_COLS = 256
PAGES_PER_STEP = 8

NT_DIMS = (((1,), (1,)), ((), ()))
TN_DIMS = (((0,), (0,)), ((), ()))


def _params(*sem):
    return pltpu.CompilerParams(dimension_semantics=sem, vmem_limit_bytes=VMEM_LIMIT_BYTES)


def _sigmoid(x):
    return 1.0 / (1.0 + jnp.exp(-x))


def _tile(n, want):
    t = min(n, want)
    while n % t:
        t -= 1
    return t


def _tile16(n, want):
    return max(t for t in range(16, min(n, want) + 1, 16) if n % t == 0)


def _ada_kernel(c_ref, w_ref, b_ref, o_ref):
    c = c_ref[...]
    a = (c * _sigmoid(c)).astype(BF16)
    o_ref[...] = jnp.dot(a, w_ref[...].astype(BF16), preferred_element_type=F32) + b_ref[...]


def _ada(c_all, w_ada, b_ada):
    depth, d, n = w_ada.shape
    rows = c_all.shape[0]
    tn = _tile(n, 512)
    return pl.pallas_call(
        _ada_kernel,
        grid=(depth, n // tn),
        in_specs=[
            pl.BlockSpec((rows, d), lambda l, j: (0, 0)),
            pl.BlockSpec((None, d, tn), lambda l, j: (l, 0, j)),
            pl.BlockSpec((None, 1, tn), lambda l, j: (l, 0, j)),
        ],
        out_specs=pl.BlockSpec((None, rows, tn), lambda l, j: (l, 0, j)),
        out_shape=jax.ShapeDtypeStruct((depth, rows, n), F32),
        compiler_params=_params("parallel", "parallel"),
        name="ada",
    )(c_all, w_ada, b_ada.reshape(depth, 1, n))


def _norm_mod_kernel(x_ref, g_ref, sc_ref, sh_ref, o_ref):
    x = x_ref[...]
    y = x * lax.rsqrt(jnp.mean(x * x, axis=-1, keepdims=True) + EPS) * g_ref[...]
    o_ref[...] = (y * (1.0 + sc_ref[...]) + sh_ref[...]).astype(o_ref.dtype)


def _norm_mod_into_kernel(x_ref, g_ref, sc_ref, sh_ref, dst_ref, o_ref):
    del dst_ref
    _norm_mod_kernel(x_ref, g_ref, sc_ref, sh_ref, o_ref)


def _norm_mod(x, gain, mods, l, m_scale, m_shift, total_rows, dst=None):
    m, d = x.shape
    groups, r = mods.shape[2], mods.shape[3]
    tm = m // groups if r > 1 else _tile(m // groups, 512)
    per_group = (m // groups) // tm
    first = 0 if dst is None else (total_rows - m) // tm
    assert dst is None or (total_rows - m) % tm == 0
    in_specs = [
        pl.BlockSpec((tm, d), lambda i: (i, 0)),
        pl.BlockSpec((None, 1, d), lambda i: (l, 0, 0)),
        pl.BlockSpec((None, None, None, r, d), lambda i: (l, m_scale, i // per_group, 0, 0)),
        pl.BlockSpec((None, None, None, r, d), lambda i: (l, m_shift, i // per_group, 0, 0)),
    ]
    args = [x, gain.reshape(gain.shape[0], 1, d), mods, mods]
    if dst is not None:
        in_specs.append(pl.BlockSpec(memory_space=pl.ANY))
        args.append(dst)
    return pl.pallas_call(
        _norm_mod_kernel if dst is None else _norm_mod_into_kernel,
        grid=(m // tm,),
        in_specs=in_specs,
        out_specs=pl.BlockSpec((tm, d), lambda i: (first + i, 0)),
        out_shape=jax.ShapeDtypeStruct((total_rows, d), BF16),
        input_output_aliases={} if dst is None else {4: 0},
        compiler_params=_params("parallel"),
        name="norm_mod",
    )(*args)


def _norm_kernel(x_ref, g_ref, o_ref):
    x = x_ref[...]
    o_ref[...] = x * lax.rsqrt(jnp.mean(x * x, axis=-1, keepdims=True) + EPS) * g_ref[...]


def _final_norm(x, gain):
    m, d = x.shape
    tm = _tile(m, 256)
    return pl.pallas_call(
        _norm_kernel,
        grid=(m // tm,),
        in_specs=[pl.BlockSpec((tm, d), lambda i: (i, 0)), pl.BlockSpec((1, d), lambda i: (0, 0))],
        out_specs=pl.BlockSpec((tm, d), lambda i: (i, 0)),
        out_shape=jax.ShapeDtypeStruct((m, d), F32),
        compiler_params=_params("parallel"),
        name="final_norm",
    )(x, gain.reshape(1, d))


def _mm_kernel(*refs, nk, epilogue):
    if epilogue == "resid":
        a_ref, w_ref, x_ref, g_ref, o_ref = refs
    else:
        a_ref, w_ref, o_ref = refs
    tn = o_ref.shape[1]
    sub = _tile(tn, MXU_COLS) if nk > 1 else tn

    def run(first, last):
        for c in range(tn // sub):
            cols = slice(c * sub, (c + 1) * sub)
            acc = jnp.dot(a_ref[...], w_ref[:, cols].astype(BF16), preferred_element_type=F32)
            if not first:
                acc = o_ref[:, cols] + acc
            if not last:
                o_ref[:, cols] = acc
            elif epilogue == "relu2":
                r = jnp.maximum(acc, 0.0)
                o_ref[:, cols] = (r * r).astype(o_ref.dtype)
            elif epilogue == "resid":
                o_ref[:, cols] = x_ref[:, cols] + g_ref[:, cols] * acc
            else:
                o_ref[:, cols] = acc.astype(o_ref.dtype)

    if nk == 1:
        run(True, True)
        return
    k = pl.program_id(2)
    pl.when(k == 0)(lambda: run(True, False))
    pl.when(jnp.logical_and(k > 0, k < nk - 1))(lambda: run(False, False))
    pl.when(k == nk - 1)(lambda: run(False, True))


def _matmul(a, w, l, n, out_dtype, tiles, epilogue="none", resid=None, mods=None, m_gate=None, rows=None):
    row0, m = rows if rows is not None else (0, a.shape[0])
    kdim = a.shape[1]
    groups = mods.shape[2] if epilogue == "resid" else 1
    tm = _tile(m // groups, tiles[0])
    tn = _tile(n, tiles[1])
    tk = _tile(kdim, tiles[2])
    nk = kdim // tk
    assert nk == 1 or out_dtype == F32
    assert row0 % tm == 0
    ib = row0 // tm
    once = pl.Buffered(1)
    big = tm * tk * a.dtype.itemsize > LARGE_BLOCK_BYTES
    in_specs = [
        pl.BlockSpec((tm, tk), lambda i, j, k: (ib + i, k), pipeline_mode=once if nk == 1 and big else None),
        pl.BlockSpec((None, tk, tn), lambda i, j, k: (l, k, j)),
    ]
    args = [a, w]
    if epilogue == "resid":
        r = mods.shape[3]
        assert r == 1 or (groups == 1 and r == tm == m)
        per_group = (m // groups) // tm
        in_specs += [
            pl.BlockSpec((tm, tn), lambda i, j, k: (i, j)),
            pl.BlockSpec((None, None, None, r, tn), lambda i, j, k: (l, m_gate, i // per_group, 0, j)),
        ]
        args += [resid, mods]
    return pl.pallas_call(
        functools.partial(_mm_kernel, nk=nk, epilogue=epilogue),
        grid=(m // tm, n // tn, nk),
        in_specs=in_specs,
        out_specs=pl.BlockSpec((tm, tn), lambda i, j, k: (i, j)),
        out_shape=jax.ShapeDtypeStruct((m, n), out_dtype),
        compiler_params=_params("parallel", "parallel", "arbitrary"),
        name="mm_" + epilogue,
    )(*args)


def _mix_kernel(oa_ref, or_ref, wpa_ref, wpr_ref, ga_ref, gr_ref, o_ref):
    pa = jnp.dot(oa_ref[...], wpa_ref[...].astype(BF16), preferred_element_type=F32)
    pr = jnp.dot(or_ref[...], wpr_ref[...].astype(BF16), preferred_element_type=F32)
    o_ref[...] = (_sigmoid(ga_ref[...]) * pa + _sigmoid(gr_ref[...]) * pr).astype(o_ref.dtype)


def _mix(o_a, o_r, w_pa, w_pr, proj, row0, l, col_ga, col_gr, d, tiles):
    m, ka = o_a.shape
    kr = o_r.shape[1]
    tm = _tile(m, tiles[0])
    tn = _tile(math.gcd(d, col_ga, col_gr), tiles[1])
    assert row0 % tm == 0
    ib = row0 // tm
    return pl.pallas_call(
        _mix_kernel,
        grid=(m // tm, d // tn),
        in_specs=[
            pl.BlockSpec((tm, ka), lambda i, j: (i, 0)),
            pl.BlockSpec((tm, kr), lambda i, j: (i, 0)),
            pl.BlockSpec((None, ka, tn), lambda i, j: (l, 0, j)),
            pl.BlockSpec((None, kr, tn), lambda i, j: (l, 0, j)),
            pl.BlockSpec((tm, tn), lambda i, j: (ib + i, col_ga // tn + j)),
            pl.BlockSpec((tm, tn), lambda i, j: (ib + i, col_gr // tn + j)),
        ],
        out_specs=pl.BlockSpec((tm, tn), lambda i, j: (i, j)),
        out_shape=jax.ShapeDtypeStruct((m, d), BF16),
        compiler_params=_params("parallel", "parallel"),
        name="mix",
    )(o_a, o_r, w_pa, w_pr, proj, proj)


def _lambda(lq1, lk1, lq2, lk2, lam_init):
    return (jnp.exp(jnp.sum(lq1[...] * lk1[...], axis=-1, keepdims=True))
            - jnp.exp(jnp.sum(lq2[...] * lk2[...], axis=-1, keepdims=True)) + lam_init)


def _subln(o, gain, lam_init):
    y = o * lax.rsqrt(jnp.mean(o * o, axis=-1, keepdims=True) + EPS) * gain
    return y * (1.0 - lam_init)


def _attn_prompt_kernel(q_ref, k_ref, v_ref, lq1, lk1, lq2, lk2, g_ref, o_ref, ko_ref, vo_ref,
                        kb, vb, s_ref, acc_ref, *, lam_init, t, tk, scale):
    qi = pl.program_id(2)
    half = HEAD_W // 2

    @pl.when(qi == 0)
    def _():
        k = k_ref[...]
        v = v_ref[...]
        ko_ref[...] = k
        vo_ref[...] = v
        kb[...] = k.astype(BF16)
        vb[...] = v.astype(BF16)

    lam = _lambda(lq1, lk1, lq2, lk2, lam_init)
    q = q_ref[...].astype(BF16)
    q1, q2 = q[:, :half], q[:, half:]

    def scores(j):
        kj = kb[pl.ds(pl.multiple_of(j * tk, tk), tk), :]
        s1 = lax.dot_general(q1, kj[:, :half], NT_DIMS, preferred_element_type=F32) * scale
        s2 = lax.dot_general(q2, kj[:, half:], NT_DIMS, preferred_element_type=F32) * scale
        return s1, s2

    def score_body(j, carry):
        m1, m2 = carry
        s1, s2 = scores(j)
        s_ref[0, j] = s1
        s_ref[1, j] = s2
        return (jnp.maximum(m1, jnp.max(s1, axis=-1, keepdims=True)),
                jnp.maximum(m2, jnp.max(s2, axis=-1, keepdims=True)))

    n_full = (qi * t) // tk
    neg = jnp.full((t, 1), -jnp.inf, F32)
    m1, m2 = lax.fori_loop(0, n_full, score_body, (neg, neg))
    s1, s2 = scores(n_full)
    q_pos = qi * t + lax.broadcasted_iota(jnp.int32, (t, tk), 0)
    k_pos = n_full * tk + lax.broadcasted_iota(jnp.int32, (t, tk), 1)
    keep = k_pos <= q_pos
    s1 = jnp.where(keep, s1, -jnp.inf)
    s2 = jnp.where(keep, s2, -jnp.inf)
    s_ref[0, n_full] = s1
    s_ref[1, n_full] = s2
    m1 = jnp.maximum(m1, jnp.max(s1, axis=-1, keepdims=True))
    m2 = jnp.maximum(m2, jnp.max(s2, axis=-1, keepdims=True))

    acc_ref[...] = jnp.zeros(acc_ref.shape, F32)

    def pv_body(j, carry):
        l1, l2 = carry
        e1 = jnp.exp(s_ref[0, j] - m1)
        e2 = jnp.exp(s_ref[1, j] - m2)
        vj = vb[pl.ds(pl.multiple_of(j * tk, tk), tk), :]
        acc_ref[0] += jnp.dot(e1.astype(BF16), vj, preferred_element_type=F32)
        acc_ref[1] += jnp.dot(e2.astype(BF16), vj, preferred_element_type=F32)
        return l1 + jnp.sum(e1, axis=-1, keepdims=True), l2 + jnp.sum(e2, axis=-1, keepdims=True)

    zero = jnp.zeros((t, 1), F32)
    l1, l2 = lax.fori_loop(0, n_full + 1, pv_body, (zero, zero))
    o = acc_ref[0] * (1.0 / l1) - acc_ref[1] * (lam / l2)
    o_ref[...] = _subln(o, g_ref[...], lam_init).astype(o_ref.dtype)


def _attn_prompt_into_kernel(*refs, **kw):
    _attn_prompt_kernel(*refs[:8], *refs[10:], **kw)


def _attn_prompt(proj, lams, subln, l, depth, kv_store, lam_init, bsz, seq, heads):
    m = bsz * seq
    tk = _tile(seq, 512)
    t = tk
    nq = seq // t
    lam_spec = pl.BlockSpec((None, 1, HEAD_W // 2), lambda b, h, i: (l, 0, 0))
    kv_spec = pl.BlockSpec((None, seq, HEAD_W), lambda b, h, i: (l, b, h))
    in_specs = [
        pl.BlockSpec((t, HEAD_W), lambda b, h, i: (b * nq + i, h)),
        pl.BlockSpec((seq, HEAD_W), lambda b, h, i: (b, heads + h)),
        pl.BlockSpec((seq, HEAD_W), lambda b, h, i: (b, 2 * heads + h)),
        lam_spec, lam_spec, lam_spec, lam_spec,
        pl.BlockSpec((None, 1, HEAD_W), lambda b, h, i: (l, 0, 0)),
    ]
    args = [proj, proj, proj, *lams, subln]
    if kv_store is not None:
        in_specs += [pl.BlockSpec(memory_space=pl.ANY)] * 2
        args += list(kv_store)
    body = _attn_prompt_kernel if kv_store is None else _attn_prompt_into_kernel
    o, k_all, v_all = pl.pallas_call(
        functools.partial(body, lam_init=lam_init, t=t, tk=tk, scale=(HEAD_W // 2) ** -0.5),
        grid=(bsz, heads, nq),
        in_specs=in_specs,
        out_specs=[pl.BlockSpec((t, HEAD_W), lambda b, h, i: (b * nq + i, h)), kv_spec, kv_spec],
        out_shape=[
            jax.ShapeDtypeStruct((m, heads * HEAD_W), BF16),
            jax.ShapeDtypeStruct((depth, m, heads * HEAD_W), F32),
            jax.ShapeDtypeStruct((depth, m, heads * HEAD_W), F32),
        ],
        input_output_aliases={} if kv_store is None else {8: 1, 9: 2},
        scratch_shapes=[
            pltpu.VMEM((seq, HEAD_W), BF16),
            pltpu.VMEM((seq, HEAD_W), BF16),
            pltpu.VMEM((2, seq // tk, t, tk), F32),
            pltpu.VMEM((2, t, HEAD_W), F32),
        ],
        compiler_params=_params("parallel", "parallel", "arbitrary"),
        name="attn_prompt",
    )(*args)
    return o, (k_all, v_all)


def _attn_decode_kernel(pt_ref, q_ref, kn_ref, vn_ref, *rest, lam_init, heads, n_new, n_pp, scale):
    kp_refs, vp_refs = rest[:n_pp], rest[n_pp:2 * n_pp]
    lq1, lk1, lq2, lk2, g_ref, o_ref, qall, bias_ref, m_ref, l_ref, acc_ref = rest[2 * n_pp:]
    p = pl.program_id(1)
    half = HEAD_W // 2
    grp = 2 * n_new
    rows = heads * grp
    page_rows = bias_ref.shape[1]

    def head_of_row(shape):
        return lax.broadcasted_iota(jnp.int32, shape, 0) // grp

    @pl.when(p == 0)
    def _():
        q = q_ref[...]
        z = jnp.zeros((n_new, half), F32)
        blocks = []
        for h in range(heads):
            q1 = q[:, h * HEAD_W:h * HEAD_W + half]
            q2 = q[:, h * HEAD_W + half:(h + 1) * HEAD_W]
            blocks.append(jnp.concatenate([q1, z], axis=1))
            blocks.append(jnp.concatenate([z, q2], axis=1))
        qall[...] = jnp.concatenate(blocks, axis=0).astype(BF16)
        shape = (rows, page_rows)
        same_head = lax.broadcasted_iota(jnp.int32, shape, 1) % heads == head_of_row(shape)
        bias_ref[...] = jnp.where(same_head, 0.0, -jnp.inf)
        m_ref[...] = jnp.full(m_ref.shape, -jnp.inf, F32)
        l_ref[...] = jnp.zeros(l_ref.shape, F32)
        acc_ref[...] = jnp.zeros(acc_ref.shape, F32)

    def absorb(keys, values, biases):
        qa = qall[...]
        ss = [lax.dot_general(qa, k.astype(BF16), NT_DIMS, preferred_element_type=F32) * scale + b
              for k, b in zip(keys, biases)]
        m_old = m_ref[...]
        m_new = m_old
        for s in ss:
            m_new = jnp.maximum(m_new, jnp.max(s, axis=-1, keepdims=True))
        alpha = jnp.exp(m_old - m_new)
        lsum = alpha * l_ref[...]
        acc = alpha * acc_ref[...]
        for s, v in zip(ss, values):
            e = jnp.exp(s - m_new)
            lsum = lsum + jnp.sum(e, axis=-1, keepdims=True)
            acc = acc + jnp.dot(e.astype(BF16), v.astype(BF16), preferred_element_type=F32)
        m_ref[...] = m_new
        l_ref[...] = lsum
        acc_ref[...] = acc

    bias = bias_ref[...]
    absorb([r[...] for r in kp_refs], [r[...] for r in vp_refs], [bias] * n_pp)

    @pl.when(p == pl.num_programs(1) - 1)
    def _():
        shape = (rows, n_new * heads)
        col = lax.broadcasted_iota(jnp.int32, shape, 1)
        tok = lax.broadcasted_iota(jnp.int32, shape, 0) % n_new
        ok = jnp.logical_and(col % heads == head_of_row(shape), col // heads <= tok)
        absorb([kn_ref[...]], [vn_ref[...]], [jnp.where(ok, 0.0, -jnp.inf)])
        lam = _lambda(lq1, lk1, lq2, lk2, lam_init)
        acc = acc_ref[...]
        lsum = l_ref[...]
        outs = []
        for h in range(heads):
            r0 = h * grp
            o = (acc[r0:r0 + n_new] / lsum[r0:r0 + n_new]
                 - lam * (acc[r0 + n_new:r0 + grp] / lsum[r0 + n_new:r0 + grp]))
            outs.append(_subln(o, g_ref[...], lam_init))
        o_ref[...] = jnp.concatenate(outs, axis=1).astype(o_ref.dtype)


def _attn_decode(proj, row0, k_new, v_new, cache_k, cache_v, page_table, lams, subln, l, lam_init, bsz, n_new, heads):
    depth, n_pool, page = cache_k.shape[:3]
    assert row0 % n_new == 0
    qb = row0 // n_new
    width = heads * HEAD_W
    n_pages = page_table.shape[1]
    n_pp = _tile(n_pages, PAGES_PER_STEP)
    page_rows = page * heads
    ck = cache_k.reshape(depth, n_pool, page_rows, HEAD_W)
    cv = cache_v.reshape(depth, n_pool, page_rows, HEAD_W)
    kn = k_new.reshape(bsz, n_new * heads, HEAD_W)
    vn = v_new.reshape(bsz, n_new * heads, HEAD_W)
    lam_spec = pl.BlockSpec((None, 1, HEAD_W // 2), lambda b, p, pt: (l, 0, 0))
    page_specs = [pl.BlockSpec((None, None, page_rows, HEAD_W), lambda b, p, pt, i=i: (l, pt[b, p * n_pp + i], 0, 0))
                  for i in range(n_pp)]
    new_spec = pl.BlockSpec((None, n_new * heads, HEAD_W), lambda b, p, pt: (b, 0, 0))
    rows = heads * 2 * n_new
    grid_spec = pltpu.PrefetchScalarGridSpec(
        num_scalar_prefetch=1,
        grid=(bsz, n_pages // n_pp),
        in_specs=[pl.BlockSpec((n_new, width), lambda b, p, pt: (qb + b, 0)), new_spec, new_spec]
        + page_specs + page_specs
        + [lam_spec, lam_spec, lam_spec, lam_spec, pl.BlockSpec((None, 1, HEAD_W), lambda b, p, pt: (l, 0, 0))],
        out_specs=pl.BlockSpec((None, n_new, width), lambda b, p, pt: (b, 0, 0)),
        scratch_shapes=[
            pltpu.VMEM((rows, HEAD_W), BF16),
            pltpu.VMEM((rows, page_rows), F32),
            pltpu.VMEM((rows, 1), F32),
            pltpu.VMEM((rows, 1), F32),
            pltpu.VMEM((rows, HEAD_W), F32),
        ],
    )
    out = pl.pallas_call(
        functools.partial(_attn_decode_kernel, lam_init=lam_init, heads=heads, n_new=n_new, n_pp=n_pp,
                          scale=(HEAD_W // 2) ** -0.5),
        grid_spec=grid_spec,
        out_shape=jax.ShapeDtypeStruct((bsz, n_new, width), BF16),
        compiler_params=_params("parallel", "arbitrary"),
        name="attn_decode",
    )(page_table, proj, kn, vn, *([ck] * n_pp), *([cv] * n_pp), *lams, subln)
    return out.reshape(bsz * n_new, width)


def _retention_kernel(*refs, nb, seq, chunk, has_state, mxu_dtype):
    if has_state:
        q_ref, k_ref, v_ref, g_ref, cos_ref, sin_ref, dm_ref, qd_ref, kd_ref, cd_ref, s0_ref, o_ref, s_ref = refs
    else:
        q_ref, k_ref, v_ref, g_ref, cos_ref, sin_ref, dm_ref, qd_ref, kd_ref, cd_ref, o_ref, s_ref = refs
    half = HEAD_W // 2
    n_chunks = seq // chunk
    k_scale = HEAD_W ** -0.5
    dmask = dm_ref[...]
    qdec = qd_ref[...]
    kdec = kd_ref[...]
    cdec = cd_ref[...]

    def rotate(x, cos, sin):
        x1, x2 = x[:, :half], x[:, half:]
        return jnp.concatenate([x1 * cos - x2 * sin, x2 * cos + x1 * sin], axis=1)

    def chunk_out(rows, pos, state):
        cos = cos_ref[pos, :]
        sin = sin_ref[pos, :]
        qc = rotate(q_ref[rows, :], cos, sin).astype(mxu_dtype)
        kf = rotate(k_ref[rows, :], cos, sin) * k_scale
        vc = v_ref[rows, :].astype(mxu_dtype)
        scores = lax.dot_general(qc, kf.astype(mxu_dtype), NT_DIMS, preferred_element_type=F32) * dmask
        o = (jnp.dot(scores.astype(mxu_dtype), vc, preferred_element_type=F32)
             + jnp.dot(qc, state.astype(mxu_dtype), preferred_element_type=F32) * qdec)
        new_state = state * cdec + lax.dot_general((kf * kdec).astype(mxu_dtype), vc, TN_DIMS,
                                                   preferred_element_type=F32)
        o = o * lax.rsqrt(jnp.mean(o * o, axis=-1, keepdims=True) + EPS)
        g = g_ref[rows, :]
        return o * (g * _sigmoid(g)), new_state

    for b in range(nb):
        state0 = s0_ref[b] if has_state else jnp.zeros((HEAD_W, HEAD_W), F32)
        if n_chunks == 1:
            out, state = chunk_out(pl.ds(b * seq, chunk), pl.ds(0, chunk), state0)
            o_ref[pl.ds(b * seq, chunk), :] = out.astype(o_ref.dtype)
        else:
            def body(c, state, b=b):
                start = pl.multiple_of(c * chunk, chunk)
                out, state = chunk_out(pl.ds(b * seq + start, chunk), pl.ds(start, chunk), state)
                o_ref[pl.ds(b * seq + start, chunk), :] = out.astype(o_ref.dtype)
                return state
            state = lax.fori_loop(0, n_chunks, body, state0, unroll=2 if n_chunks % 2 == 0 else 1)
        s_ref[b] = state


def _retention(proj, row0, state0, l, tables, bsz, seq, heads, col_q, nb, out_dtype, mxu_dtype):
    cos, sin, dmask, qdec, kdec, cdec = tables
    m = bsz * seq
    chunk = dmask.shape[1]
    cb = col_q // HEAD_W
    rows = nb * seq
    assert row0 % rows == 0
    rb = row0 // rows

    def col(which):
        return pl.BlockSpec((rows, HEAD_W), lambda g, h: (rb + g, cb + which * heads + h))

    in_specs = [
        col(0), col(1), col(2), col(3),
        pl.BlockSpec((seq, HEAD_W // 2), lambda g, h: (0, 0)),
        pl.BlockSpec((seq, HEAD_W // 2), lambda g, h: (0, 0)),
        pl.BlockSpec((None, chunk, chunk), lambda g, h: (h, 0, 0)),
        pl.BlockSpec((None, chunk, 1), lambda g, h: (h, 0, 0)),
        pl.BlockSpec((None, chunk, 1), lambda g, h: (h, 0, 0)),
        pl.BlockSpec((None, 1, 1), lambda g, h: (h, 0, 0)),
    ]
    args = [proj, proj, proj, proj, cos, sin, dmask, qdec, kdec, cdec]
    has_state = state0 is not None
    if has_state:
        in_specs.append(pl.BlockSpec((None, nb, None, HEAD_W, HEAD_W), lambda g, h: (l, g, h, 0, 0)))
        args.append(state0)
    return pl.pallas_call(
        functools.partial(_retention_kernel, nb=nb, seq=seq, chunk=chunk, has_state=has_state, mxu_dtype=mxu_dtype),
        grid=(bsz // nb, heads),
        in_specs=in_specs,
        out_specs=[
            pl.BlockSpec((rows, HEAD_W), lambda g, h: (g, h)),
            pl.BlockSpec((nb, None, HEAD_W, HEAD_W), lambda g, h: (g, h, 0, 0)),
        ],
        out_shape=[
            jax.ShapeDtypeStruct((m, heads * HEAD_W), out_dtype),
            jax.ShapeDtypeStruct((bsz, heads, HEAD_W, HEAD_W), F32),
        ],
        compiler_params=_params("parallel", "parallel"),
        name="retention",
    )(*args)


def _retention_tables(seq, first_pos, heads):
    half = HEAD_W // 2
    inv = 1.0 / (ROPE_BASE ** jnp.linspace(0.0, 1.0, half, dtype=F32))
    pos = (first_pos + jnp.arange(seq, dtype=jnp.int32)).astype(F32)
    ang = pos[:, None] * inv[None, :]
    chunk = seq if seq <= RET_CHUNK else RET_CHUNK
    lg = jnp.log(1.0 - 2.0 ** (-5.0 - jnp.arange(heads, dtype=F32)))
    idx = jnp.arange(chunk, dtype=F32)
    rel = idx[:, None] - idx[None, :]
    causal = rel >= 0
    dmask = jnp.where(causal[None], jnp.exp(jnp.where(causal, rel, 0.0)[None] * lg[:, None, None]), 0.0)
    qdec = jnp.exp((idx + 1.0)[None, :] * lg[:, None])[:, :, None]
    kdec = jnp.exp((chunk - 1.0 - idx)[None, :] * lg[:, None])[:, :, None]
    cdec = jnp.exp(chunk * lg)[:, None, None]
    return jnp.cos(ang), jnp.sin(ang), dmask, qdec, kdec, cdec


def kernel(x_prompt, x_sample, cache_k, cache_v, state_ret, page_table, c_prompt, c_sample, w_ada, b_ada, norm1, norm2, w_in, lambda_q1, lambda_k1, lambda_q2, lambda_k2, da_subln, w_pa, w_pr, w_o, w_up, w_down, norm_f):
    bp, seq, d = x_prompt.shape
    bs, n_new, _ = x_sample.shape
    depth = w_in.shape[0]
    page, heads = cache_k.shape[2], cache_k.shape[3]
    r_heads = state_ret.shape[2]
    assert cache_k.shape[4] == cache_v.shape[4] == state_ret.shape[3] == state_ret.shape[4] == HEAD_W
    assert heads == r_heads
    past_len = page_table.shape[1] * page
    d_ff = w_up.shape[2]
    hw = heads * HEAD_W
    col_k, col_v, col_qr = hw, 2 * hw, 3 * hw
    col_ga = 7 * hw

    c_rows = -(-(bp + bs) // 8) * 8
    c_all = jnp.concatenate([c_prompt, c_sample, jnp.zeros((c_rows - bp - bs, d), F32)], axis=0)
    mod = _ada(c_all, w_ada, b_ada)
    mod_p = mod[:, :bp].reshape(depth, bp, N_MOD, 1, d).transpose(0, 2, 1, 3, 4)
    mod_s = mod[:, bp:bp + bs].reshape(depth, bs, N_MOD, d).transpose(0, 2, 1, 3)
    mod_s = jnp.repeat(mod_s, n_new, axis=2)[:, :, None]

    lams = [a.reshape(depth, 1, -1) for a in (lambda_q1, lambda_k1, lambda_q2, lambda_k2)]
    subln = da_subln.reshape(depth, 1, HEAD_W)
    tab_p = _retention_tables(seq, 0, heads)
    tab_s = _retention_tables(n_new, past_len, heads)

    mp, ms = bp * seq, bs * n_new
    rows_all = mp + ms
    wide_all = (_tile16(rows_all, ROW_BLOCK_MAX),) + TILES_PROMPT_WIDE[1:]

    def norm_both(xp, xs, gain, l, m_scale, m_shift):
        h = _norm_mod(xp, gain, mod_p, l, m_scale, m_shift, rows_all)
        return _norm_mod(xs, gain, mod_s, l, m_scale, m_shift, rows_all, dst=h)

    xp = x_prompt.reshape(mp, d)
    xs = x_sample.reshape(ms, d)
    sp, ksm, vsm, ssm = [], [], [], []
    kv_p = None
    for l in range(depth):
        lam_init = 0.8 - 0.6 * math.exp(-0.3 * l)

        h = norm_both(xp, xs, norm1, l, 1, 0)
        proj = _matmul(h, w_in, l, w_in.shape[2], F32, wide_all)

        o_a_p, kv_p = _attn_prompt(proj, lams, subln, l, depth, kv_p, lam_init, bp, seq, heads)
        o_r_p, s_p = _retention(proj, 0, None, l, tab_p, bp, seq, heads, col_qr, 1, BF16, BF16)
        sp.append(s_p)

        k_s = proj[mp:, col_k:col_k + hw].reshape(bs, n_new, heads, HEAD_W)
        v_s = proj[mp:, col_v:col_v + hw].reshape(bs, n_new, heads, HEAD_W)
        o_a_s = _attn_decode(proj, mp, k_s, v_s, cache_k, cache_v, page_table, lams, subln, l, lam_init,
                             bs, n_new, heads)
        o_r_s, s_s = _retention(proj, mp, state_ret, l, tab_s, bs, n_new, heads, col_qr, bs, F32, F32)
        ksm.append(k_s)
        vsm.append(v_s)
        ssm.append(s_s)

        mixed_p = _mix(o_a_p, o_r_p, w_pa, w_pr, proj, 0, l, col_ga, col_ga + d, d, TILES_PROMPT_SQUARE)
        mixed_s = _mix(o_a_s, o_r_s.astype(BF16), w_pa, w_pr, proj, mp, l, col_ga, col_ga + d, d, TILES_DECODE)
        xp = _matmul(mixed_p, w_o, l, d, F32, TILES_PROMPT_SQUARE, "resid", xp, mod_p, 2)
        xs = _matmul(mixed_s, w_o, l, d, F32, TILES_DECODE, "resid", xs, mod_s, 2)

        h2 = norm_both(xp, xs, norm2, l, 4, 3)
        u = _matmul(h2, w_up, l, d_ff, BF16, wide_all, "relu2")
        xp = _matmul(u, w_down, l, d, F32, TILES_PROMPT_DEEP, "resid", xp, mod_p, 5, rows=(0, mp))
        xs = _matmul(u, w_down, l, d, F32, TILES_DECODE, "resid", xs, mod_s, 5, rows=(mp, ms))

    y_prompt = _final_norm(xp, norm_f).reshape(bp, seq, d)
    y_sample = _final_norm(xs, norm_f).reshape(bs, n_new, d)
    new_k_p, new_v_p = (a.reshape(depth, bp, seq, heads, HEAD_W) for a in kv_p)
    return (y_prompt, y_sample, new_k_p, new_v_p, jnp.stack(sp),
            jnp.stack(ksm), jnp.stack(vsm), jnp.stack(ssm))
```

```python
import functools
import math

import jax
import jax.numpy as jnp
from jax import lax
from jax.experimental import pallas as pl
from jax.experimental.pallas import tpu as pltpu

F32 = jnp.float32
BF16 = jnp.bfloat16

EPS = 1e-6
ROPE_BASE = 10000.0
RET_CHUNK = 128
N_MOD = 6
HEAD_W = 256
VMEM_LIMIT_BYTES = 60 * 1024 * 1024

LARGE_BLOCK_BYTES = 12 * 1024 * 1024
ROW_BLOCK_MAX = 2304
TILES_PROMPT_WIDE = (2048, 512, 4096)
TILES_PROMPT_SQUARE = (1024, 512, 4096)
TILES_PROMPT_DEEP = (2048, 1024, 1024)
TILES_DECODE = (64, 1024, 4096)
MXU_COLS = 256
PAGES_PER_STEP = 8
ATTN_CHUNK = 512
NORM_ROWS = 512
ADA_COLS = 512
RET_UNROLL = 4

NT_DIMS = (((1,), (1,)), ((), ()))
TN_DIMS = (((0,), (0,)), ((), ()))


def _params(*sem):
    return pltpu.CompilerParams(dimension_semantics=sem, vmem_limit_bytes=VMEM_LIMIT_BYTES)


def _sigmoid(x):
    return 1.0 / (1.0 + jnp.exp(-x))


def _tile(n, want):
    t = min(n, want)
    while n % t:
        t -= 1
    return t


def _tile16(n, want):
    return max(t for t in range(16, min(n, want) + 1, 16) if n % t == 0)


def _ada_kernel(c_ref, w_ref, b_ref, o_ref):
    c = c_ref[...]
    a = (c * _sigmoid(c)).astype(BF16)
    o_ref[...] = jnp.dot(a, w_ref[...].astype(BF16), preferred_element_type=F32) + b_ref[...]


def _ada(c_all, w_ada, b_ada):
    depth, d, n = w_ada.shape
    rows = c_all.shape[0]
    tn = _tile(n, ADA_COLS)
    return pl.pallas_call(
        _ada_kernel,
        grid=(depth, n // tn),
        in_specs=[
            pl.BlockSpec((rows, d), lambda l, j: (0, 0)),
            pl.BlockSpec((None, d, tn), lambda l, j: (l, 0, j)),
            pl.BlockSpec((None, 1, tn), lambda l, j: (l, 0, j)),
        ],
        out_specs=pl.BlockSpec((None, rows, tn), lambda l, j: (l, 0, j)),
        out_shape=jax.ShapeDtypeStruct((depth, rows, n), F32),
        compiler_params=_params("parallel", "parallel"),
        name="ada",
    )(c_all, w_ada, b_ada.reshape(depth, 1, n))


def _norm_mod_kernel(x_ref, g_ref, sc_ref, sh_ref, o_ref):
    x = x_ref[...]
    y = x * lax.rsqrt(jnp.mean(x * x, axis=-1, keepdims=True) + EPS) * g_ref[...]
    o_ref[...] = (y * (1.0 + sc_ref[...]) + sh_ref[...]).astype(o_ref.dtype)


def _norm_mod_into_kernel(x_ref, g_ref, sc_ref, sh_ref, dst_ref, o_ref):
    del dst_ref
    _norm_mod_kernel(x_ref, g_ref, sc_ref, sh_ref, o_ref)


def _norm_mod(x, gain, mods, l, m_scale, m_shift, total_rows, dst=None):
    m, d = x.shape
    groups, r = mods.shape[2], mods.shape[3]
    tm = m // groups if r > 1 else _tile(m // groups, NORM_ROWS)
    per_group = (m // groups) // tm
    first = 0 if dst is None else (total_rows - m) // tm
    assert dst is None or (total_rows - m) % tm == 0
    in_specs = [
        pl.BlockSpec((tm, d), lambda i: (i, 0)),
        pl.BlockSpec((None, 1, d), lambda i: (l, 0, 0)),
        pl.BlockSpec((None, None, None, r, d), lambda i: (l, m_scale, i // per_group, 0, 0)),
        pl.BlockSpec((None, None, None, r, d), lambda i: (l, m_shift, i // per_group, 0, 0)),
    ]
    args = [x, gain.reshape(gain.shape[0], 1, d), mods, mods]
    if dst is not None:
        in_specs.append(pl.BlockSpec(memory_space=pl.ANY))
        args.append(dst)
    return pl.pallas_call(
        _norm_mod_kernel if dst is None else _norm_mod_into_kernel,
        grid=(m // tm,),
        in_specs=in_specs,
        out_specs=pl.BlockSpec((tm, d), lambda i: (first + i, 0)),
        out_shape=jax.ShapeDtypeStruct((total_rows, d), BF16),
        input_output_aliases={} if dst is None else {4: 0},
        compiler_params=_params("parallel"),
        name="norm_mod",
    )(*args)


def _norm_kernel(x_ref, g_ref, o_ref):
    x = x_ref[...]
    o_ref[...] = x * lax.rsqrt(jnp.mean(x * x, axis=-1, keepdims=True) + EPS) * g_ref[...]


def _final_norm(x, gain):
    m, d = x.shape
    tm = _tile(m, NORM_ROWS)
    return pl.pallas_call(
        _norm_kernel,
        grid=(m // tm,),
        in_specs=[pl.BlockSpec((tm, d), lambda i: (i, 0)), pl.BlockSpec((1, d), lambda i: (0, 0))],
        out_specs=pl.BlockSpec((tm, d), lambda i: (i, 0)),
        out_shape=jax.ShapeDtypeStruct((m, d), F32),
        compiler_params=_params("parallel"),
        name="final_norm",
    )(x, gain.reshape(1, d))


def _mm_kernel(*refs, nk, epilogue):
    if epilogue == "resid":
        a_ref, w_ref, x_ref, g_ref, o_ref = refs
    else:
        a_ref, w_ref, o_ref = refs
    tn = o_ref.shape[1]
    sub = _tile(tn, MXU_COLS) if nk > 1 else tn

    def run(first, last):
        for c in range(tn // sub):
            cols = slice(c * sub, (c + 1) * sub)
            acc = jnp.dot(a_ref[...], w_ref[:, cols].astype(BF16), preferred_element_type=F32)
            if not first:
                acc = o_ref[:, cols] + acc
            if not last:
                o_ref[:, cols] = acc
            elif epilogue == "relu2":
                r = jnp.maximum(acc, 0.0)
                o_ref[:, cols] = (r * r).astype(o_ref.dtype)
            elif epilogue == "resid":
                o_ref[:, cols] = x_ref[:, cols] + g_ref[:, cols] * acc
            else:
                o_ref[:, cols] = acc.astype(o_ref.dtype)

    if nk == 1:
        run(True, True)
        return
    k = pl.program_id(2)
    pl.when(k == 0)(lambda: run(True, False))
    pl.when(jnp.logical_and(k > 0, k < nk - 1))(lambda: run(False, False))
    pl.when(k == nk - 1)(lambda: run(False, True))


def _matmul(a, w, l, n, out_dtype, tiles, epilogue="none", resid=None, mods=None, m_gate=None, rows=None):
    row0, m = rows if rows is not None else (0, a.shape[0])
    kdim = a.shape[1]
    groups = mods.shape[2] if epilogue == "resid" else 1
    tm = _tile(m // groups, tiles[0])
    tn = _tile(n, tiles[1])
    tk = _tile(kdim, tiles[2])
    nk = kdim // tk
    assert nk == 1 or out_dtype == F32
    assert row0 % tm == 0
    ib = row0 // tm
    once = pl.Buffered(1)
    big = tm * tk * a.dtype.itemsize > LARGE_BLOCK_BYTES
    in_specs = [
        pl.BlockSpec((tm, tk), lambda i, j, k: (ib + i, k), pipeline_mode=once if nk == 1 and big else None),
        pl.BlockSpec((None, tk, tn), lambda i, j, k: (l, k, j)),
    ]
    args = [a, w]
    if epilogue == "resid":
        r = mods.shape[3]
        assert r == 1 or (groups == 1 and r == tm == m)
        per_group = (m // groups) // tm
        in_specs += [
            pl.BlockSpec((tm, tn), lambda i, j, k: (i, j)),
            pl.BlockSpec((None, None, None, r, tn), lambda i, j, k: (l, m_gate, i // per_group, 0, j)),
        ]
        args += [resid, mods]
    return pl.pallas_call(
        functools.partial(_mm_kernel, nk=nk, epilogue=epilogue),
        grid=(m // tm, n // tn, nk),
        in_specs=in_specs,
        out_specs=pl.BlockSpec((tm, tn), lambda i, j, k: (i, j)),
        out_shape=jax.ShapeDtypeStruct((m, n), out_dtype),
        compiler_params=_params("parallel", "parallel", "arbitrary"),
        name="mm_" + epilogue,
    )(*args)


def _mix_kernel(oa_ref, or_ref, wpa_ref, wpr_ref, ga_ref, gr_ref, o_ref):
    pa = jnp.dot(oa_ref[...], wpa_ref[...].astype(BF16), preferred_element_type=F32)
    pr = jnp.dot(or_ref[...], wpr_ref[...].astype(BF16), preferred_element_type=F32)
    o_ref[...] = (_sigmoid(ga_ref[...]) * pa + _sigmoid(gr_ref[...]) * pr).astype(o_ref.dtype)


def _mix(o_a, o_r, w_pa, w_pr, proj, row0, l, col_ga, col_gr, d, tiles):
    m, ka = o_a.shape
    kr = o_r.shape[1]
    tm = _tile(m, tiles[0])
    tn = _tile(math.gcd(d, col_ga, col_gr), tiles[1])
    assert row0 % tm == 0
    ib = row0 // tm
    return pl.pallas_call(
        _mix_kernel,
        grid=(m // tm, d // tn),
        in_specs=[
            pl.BlockSpec((tm, ka), lambda i, j: (i, 0)),
            pl.BlockSpec((tm, kr), lambda i, j: (i, 0)),
            pl.BlockSpec((None, ka, tn), lambda i, j: (l, 0, j)),
            pl.BlockSpec((None, kr, tn), lambda i, j: (l, 0, j)),
            pl.BlockSpec((tm, tn), lambda i, j: (ib + i, col_ga // tn + j)),
            pl.BlockSpec((tm, tn), lambda i, j: (ib + i, col_gr // tn + j)),
        ],
        out_specs=pl.BlockSpec((tm, tn), lambda i, j: (i, j)),
        out_shape=jax.ShapeDtypeStruct((m, d), BF16),
        compiler_params=_params("parallel", "parallel"),
        name="mix",
    )(o_a, o_r, w_pa, w_pr, proj, proj)


def _lambda(lq1, lk1, lq2, lk2, lam_init):
    return (jnp.exp(jnp.sum(lq1[...] * lk1[...], axis=-1, keepdims=True))
            - jnp.exp(jnp.sum(lq2[...] * lk2[...], axis=-1, keepdims=True)) + lam_init)


def _subln(o, gain, lam_init):
    y = o * lax.rsqrt(jnp.mean(o * o, axis=-1, keepdims=True) + EPS) * gain
    return y * (1.0 - lam_init)


def _attn_prompt_kernel(q_ref, k_ref, v_ref, lq1, lk1, lq2, lk2, g_ref, o_ref, ko_ref, vo_ref,
                        kb, vb, s_ref, acc_ref, *, lam_init, t, tk, scale):
    qi = pl.program_id(2)
    half = HEAD_W // 2

    @pl.when(qi == 0)
    def _():
        k = k_ref[...]
        v = v_ref[...]
        ko_ref[...] = k
        vo_ref[...] = v
        kb[...] = k.astype(BF16)
        vb[...] = v.astype(BF16)

    lam = _lambda(lq1, lk1, lq2, lk2, lam_init)
    q = q_ref[...].astype(BF16)
    q1, q2 = q[:, :half], q[:, half:]

    def scores(j):
        kj = kb[pl.ds(pl.multiple_of(j * tk, tk), tk), :]
        s1 = lax.dot_general(q1, kj[:, :half], NT_DIMS, preferred_element_type=F32) * scale
        s2 = lax.dot_general(q2, kj[:, half:], NT_DIMS, preferred_element_type=F32) * scale
        return s1, s2

    def score_body(j, carry):
        m1, m2 = carry
        s1, s2 = scores(j)
        s_ref[0, j] = s1
        s_ref[1, j] = s2
        return (jnp.maximum(m1, jnp.max(s1, axis=-1, keepdims=True)),
                jnp.maximum(m2, jnp.max(s2, axis=-1, keepdims=True)))

    n_full = (qi * t) // tk
    neg = jnp.full((t, 1), -jnp.inf, F32)
    m1, m2 = lax.fori_loop(0, n_full, score_body, (neg, neg))
    s1, s2 = scores(n_full)
    q_pos = qi * t + lax.broadcasted_iota(jnp.int32, (t, tk), 0)
    k_pos = n_full * tk + lax.broadcasted_iota(jnp.int32, (t, tk), 1)
    keep = k_pos <= q_pos
    s1 = jnp.where(keep, s1, -jnp.inf)
    s2 = jnp.where(keep, s2, -jnp.inf)
    s_ref[0, n_full] = s1
    s_ref[1, n_full] = s2
    m1 = jnp.maximum(m1, jnp.max(s1, axis=-1, keepdims=True))
    m2 = jnp.maximum(m2, jnp.max(s2, axis=-1, keepdims=True))

    acc_ref[...] = jnp.zeros(acc_ref.shape, F32)

    def pv_body(j, carry):
        l1, l2 = carry
        e1 = jnp.exp(s_ref[0, j] - m1)
        e2 = jnp.exp(s_ref[1, j] - m2)
        vj = vb[pl.ds(pl.multiple_of(j * tk, tk), tk), :]
        acc_ref[0] += jnp.dot(e1.astype(BF16), vj, preferred_element_type=F32)
        acc_ref[1] += jnp.dot(e2.astype(BF16), vj, preferred_element_type=F32)
        return l1 + jnp.sum(e1, axis=-1, keepdims=True), l2 + jnp.sum(e2, axis=-1, keepdims=True)

    zero = jnp.zeros((t, 1), F32)
    l1, l2 = lax.fori_loop(0, n_full + 1, pv_body, (zero, zero))
    o = acc_ref[0] * (1.0 / l1) - acc_ref[1] * (lam / l2)
    o_ref[...] = _subln(o, g_ref[...], lam_init).astype(o_ref.dtype)


def _attn_prompt_into_kernel(*refs, **kw):
    _attn_prompt_kernel(*refs[:8], *refs[10:], **kw)


def _attn_prompt(proj, lams, subln, l, depth, kv_store, lam_init, bsz, seq, heads):
    m = bsz * seq
    tk = _tile(seq, ATTN_CHUNK)
    t = tk
    nq = seq // t
    lam_spec = pl.BlockSpec((None, 1, HEAD_W // 2), lambda b, h, i: (l, 0, 0))
    kv_spec = pl.BlockSpec((None, seq, HEAD_W), lambda b, h, i: (l, b, h))
    in_specs = [
        pl.BlockSpec((t, HEAD_W), lambda b, h, i: (b * nq + i, h)),
        pl.BlockSpec((seq, HEAD_W), lambda b, h, i: (b, heads + h)),
        pl.BlockSpec((seq, HEAD_W), lambda b, h, i: (b, 2 * heads + h)),
        lam_spec, lam_spec, lam_spec, lam_spec,
        pl.BlockSpec((None, 1, HEAD_W), lambda b, h, i: (l, 0, 0)),
    ]
    args = [proj, proj, proj, *lams, subln]
    if kv_store is not None:
        in_specs += [pl.BlockSpec(memory_space=pl.ANY)] * 2
        args += list(kv_store)
    body = _attn_prompt_kernel if kv_store is None else _attn_prompt_into_kernel
    o, k_all, v_all = pl.pallas_call(
        functools.partial(body, lam_init=lam_init, t=t, tk=tk, scale=(HEAD_W // 2) ** -0.5),
        grid=(bsz, heads, nq),
        in_specs=in_specs,
        out_specs=[pl.BlockSpec((t, HEAD_W), lambda b, h, i: (b * nq + i, h)), kv_spec, kv_spec],
        out_shape=[
            jax.ShapeDtypeStruct((m, heads * HEAD_W), BF16),
            jax.ShapeDtypeStruct((depth, m, heads * HEAD_W), F32),
            jax.ShapeDtypeStruct((depth, m, heads * HEAD_W), F32),
        ],
        input_output_aliases={} if kv_store is None else {8: 1, 9: 2},
        scratch_shapes=[
            pltpu.VMEM((seq, HEAD_W), BF16),
            pltpu.VMEM((seq, HEAD_W), BF16),
            pltpu.VMEM((2, seq // tk, t, tk), F32),
            pltpu.VMEM((2, t, HEAD_W), F32),
        ],
        compiler_params=_params("parallel", "parallel", "arbitrary"),
        name="attn_prompt",
    )(*args)
    return o, (k_all, v_all)


def _attn_decode_kernel(pt_ref, q_ref, kn_ref, vn_ref, *rest, lam_init, heads, n_new, n_pp, scale):
    kp_refs, vp_refs = rest[:n_pp], rest[n_pp:2 * n_pp]
    lq1, lk1, lq2, lk2, g_ref, o_ref, qall, bias_ref, m_ref, l_ref, acc_ref = rest[2 * n_pp:]
    p = pl.program_id(1)
    half = HEAD_W // 2
    grp = 2 * n_new
    rows = heads * grp
    page_rows = bias_ref.shape[1]

    def head_of_row(shape):
        return lax.broadcasted_iota(jnp.int32, shape, 0) // grp

    @pl.when(p == 0)
    def _():
        q = q_ref[...]
        z = jnp.zeros((n_new, half), F32)
        blocks = []
        for h in range(heads):
            q1 = q[:, h * HEAD_W:h * HEAD_W + half]
            q2 = q[:, h * HEAD_W + half:(h + 1) * HEAD_W]
            blocks.append(jnp.concatenate([q1, z], axis=1))
            blocks.append(jnp.concatenate([z, q2], axis=1))
        qall[...] = jnp.concatenate(blocks, axis=0).astype(BF16)
        shape = (rows, page_rows)
        same_head = lax.broadcasted_iota(jnp.int32, shape, 1) % heads == head_of_row(shape)
        bias_ref[...] = jnp.where(same_head, 0.0, -jnp.inf)
        m_ref[...] = jnp.full(m_ref.shape, -jnp.inf, F32)
        l_ref[...] = jnp.zeros(l_ref.shape, F32)
        acc_ref[...] = jnp.zeros(acc_ref.shape, F32)

    def absorb(keys, values, biases):
        qa = qall[...]
        ss = [lax.dot_general(qa, k.astype(BF16), NT_DIMS, preferred_element_type=F32) * scale + b
              for k, b in zip(keys, biases)]
        m_old = m_ref[...]
        m_new = m_old
        for s in ss:
            m_new = jnp.maximum(m_new, jnp.max(s, axis=-1, keepdims=True))
        alpha = jnp.exp(m_old - m_new)
        lsum = alpha * l_ref[...]
        acc = alpha * acc_ref[...]
        for s, v in zip(ss, values):
            e = jnp.exp(s - m_new)
            lsum = lsum + jnp.sum(e, axis=-1, keepdims=True)
            acc = acc + jnp.dot(e.astype(BF16), v.astype(BF16), preferred_element_type=F32)
        m_ref[...] = m_new
        l_ref[...] = lsum
        acc_ref[...] = acc

    bias = bias_ref[...]
    absorb([r[...] for r in kp_refs], [r[...] for r in vp_refs], [bias] * n_pp)

    @pl.when(p == pl.num_programs(1) - 1)
    def _():
        shape = (rows, n_new * heads)
        col = lax.broadcasted_iota(jnp.int32, shape, 1)
        tok = lax.broadcasted_iota(jnp.int32, shape, 0) % n_new
        ok = jnp.logical_and(col % heads == head_of_row(shape), col // heads <= tok)
        absorb([kn_ref[...]], [vn_ref[...]], [jnp.where(ok, 0.0, -jnp.inf)])
        lam = _lambda(lq1, lk1, lq2, lk2, lam_init)
        acc = acc_ref[...]
        lsum = l_ref[...]
        outs = []
        for h in range(heads):
            r0 = h * grp
            o = (acc[r0:r0 + n_new] / lsum[r0:r0 + n_new]
                 - lam * (acc[r0 + n_new:r0 + grp] / lsum[r0 + n_new:r0 + grp]))
            outs.append(_subln(o, g_ref[...], lam_init))
        o_ref[...] = jnp.concatenate(outs, axis=1).astype(o_ref.dtype)


def _attn_decode(proj, row0, k_new, v_new, cache_k, cache_v, page_table, lams, subln, l, lam_init, bsz, n_new, heads):
    depth, n_pool, page = cache_k.shape[:3]
    assert row0 % n_new == 0
    qb = row0 // n_new
    width = heads * HEAD_W
    n_pages = page_table.shape[1]
    n_pp = _tile(n_pages, PAGES_PER_STEP)
    page_rows = page * heads
    ck = cache_k.reshape(depth, n_pool, page_rows, HEAD_W)
    cv = cache_v.reshape(depth, n_pool, page_rows, HEAD_W)
    kn = k_new.reshape(bsz, n_new * heads, HEAD_W)
    vn = v_new.reshape(bsz, n_new * heads, HEAD_W)
    lam_spec = pl.BlockSpec((None, 1, HEAD_W // 2), lambda b, p, pt: (l, 0, 0))
    page_specs = [pl.BlockSpec((None, None, page_rows, HEAD_W), lambda b, p, pt, i=i: (l, pt[b, p * n_pp + i], 0, 0))
                  for i in range(n_pp)]
    new_spec = pl.BlockSpec((None, n_new * heads, HEAD_W), lambda b, p, pt: (b, 0, 0))
    rows = heads * 2 * n_new
    grid_spec = pltpu.PrefetchScalarGridSpec(
        num_scalar_prefetch=1,
        grid=(bsz, n_pages // n_pp),
        in_specs=[pl.BlockSpec((n_new, width), lambda b, p, pt: (qb + b, 0)), new_spec, new_spec]
        + page_specs + page_specs
        + [lam_spec, lam_spec, lam_spec, lam_spec, pl.BlockSpec((None, 1, HEAD_W), lambda b, p, pt: (l, 0, 0))],
        out_specs=pl.BlockSpec((None, n_new, width), lambda b, p, pt: (b, 0, 0)),
        scratch_shapes=[
            pltpu.VMEM((rows, HEAD_W), BF16),
            pltpu.VMEM((rows, page_rows), F32),
            pltpu.VMEM((rows, 1), F32),
            pltpu.VMEM((rows, 1), F32),
            pltpu.VMEM((rows, HEAD_W), F32),
        ],
    )
    out = pl.pallas_call(
        functools.partial(_attn_decode_kernel, lam_init=lam_init, heads=heads, n_new=n_new, n_pp=n_pp,
                          scale=(HEAD_W // 2) ** -0.5),
        grid_spec=grid_spec,
        out_shape=jax.ShapeDtypeStruct((bsz, n_new, width), BF16),
        compiler_params=_params("parallel", "arbitrary"),
        name="attn_decode",
    )(page_table, proj, kn, vn, *([ck] * n_pp), *([cv] * n_pp), *lams, subln)
    return out.reshape(bsz * n_new, width)


def _retention_kernel(*refs, nb, seq, chunk, has_state, mxu_dtype):
    if has_state:
        q_ref, k_ref, v_ref, g_ref, cos_ref, sin_ref, dm_ref, qd_ref, kd_ref, cd_ref, s0_ref, o_ref, s_ref = refs
    else:
        q_ref, k_ref, v_ref, g_ref, cos_ref, sin_ref, dm_ref, qd_ref, kd_ref, cd_ref, o_ref, s_ref = refs
    half = HEAD_W // 2
    n_chunks = seq // chunk
    k_scale = HEAD_W ** -0.5
    dmask = dm_ref[...]
    qdec = qd_ref[...]
    kdec = kd_ref[...]
    cdec = cd_ref[...]

    def rotate(x, cos, sin):
        x1, x2 = x[:, :half], x[:, half:]
        return jnp.concatenate([x1 * cos - x2 * sin, x2 * cos + x1 * sin], axis=1)

    def chunk_out(rows, pos, state):
        cos = cos_ref[pos, :]
        sin = sin_ref[pos, :]
        qc = rotate(q_ref[rows, :], cos, sin).astype(mxu_dtype)
        kf = rotate(k_ref[rows, :], cos, sin) * k_scale
        vc = v_ref[rows, :].astype(mxu_dtype)
        scores = lax.dot_general(qc, kf.astype(mxu_dtype), NT_DIMS, preferred_element_type=F32) * dmask
        o = (jnp.dot(scores.astype(mxu_dtype), vc, preferred_element_type=F32)
             + jnp.dot(qc, state.astype(mxu_dtype), preferred_element_type=F32) * qdec)
        new_state = state * cdec + lax.dot_general((kf * kdec).astype(mxu_dtype), vc, TN_DIMS,
                                                   preferred_element_type=F32)
        o = o * lax.rsqrt(jnp.mean(o * o, axis=-1, keepdims=True) + EPS)
        g = g_ref[rows, :]
        return o * (g * _sigmoid(g)), new_state

    for b in range(nb):
        state0 = s0_ref[b] if has_state else jnp.zeros((HEAD_W, HEAD_W), F32)
        if n_chunks == 1:
            out, state = chunk_out(pl.ds(b * seq, chunk), pl.ds(0, chunk), state0)
            o_ref[pl.ds(b * seq, chunk), :] = out.astype(o_ref.dtype)
        else:
            def body(c, state, b=b):
                start = pl.multiple_of(c * chunk, chunk)
                out, state = chunk_out(pl.ds(b * seq + start, chunk), pl.ds(start, chunk), state)
                o_ref[pl.ds(b * seq + start, chunk), :] = out.astype(o_ref.dtype)
                return state
            state = lax.fori_loop(0, n_chunks, body, state0, unroll=math.gcd(n_chunks, RET_UNROLL))
        s_ref[b] = state


def _retention(proj, row0, state0, l, tables, bsz, seq, heads, col_q, nb, out_dtype, mxu_dtype):
    cos, sin, dmask, qdec, kdec, cdec = tables
    m = bsz * seq
    chunk = dmask.shape[1]
    cb = col_q // HEAD_W
    rows = nb * seq
    assert row0 % rows == 0
    rb = row0 // rows

    def col(which):
        return pl.BlockSpec((rows, HEAD_W), lambda g, h: (rb + g, cb + which * heads + h))

    in_specs = [
        col(0), col(1), col(2), col(3),
        pl.BlockSpec((seq, HEAD_W // 2), lambda g, h: (0, 0)),
        pl.BlockSpec((seq, HEAD_W // 2), lambda g, h: (0, 0)),
        pl.BlockSpec((None, chunk, chunk), lambda g, h: (h, 0, 0)),
        pl.BlockSpec((None, chunk, 1), lambda g, h: (h, 0, 0)),
        pl.BlockSpec((None, chunk, 1), lambda g, h: (h, 0, 0)),
        pl.BlockSpec((None, 1, 1), lambda g, h: (h, 0, 0)),
    ]
    args = [proj, proj, proj, proj, cos, sin, dmask, qdec, kdec, cdec]
    has_state = state0 is not None
    if has_state:
        in_specs.append(pl.BlockSpec((None, nb, None, HEAD_W, HEAD_W), lambda g, h: (l, g, h, 0, 0)))
        args.append(state0)
    return pl.pallas_call(
        functools.partial(_retention_kernel, nb=nb, seq=seq, chunk=chunk, has_state=has_state, mxu_dtype=mxu_dtype),
        grid=(bsz // nb, heads),
        in_specs=in_specs,
        out_specs=[
            pl.BlockSpec((rows, HEAD_W), lambda g, h: (g, h)),
            pl.BlockSpec((nb, None, HEAD_W, HEAD_W), lambda g, h: (g, h, 0, 0)),
        ],
        out_shape=[
            jax.ShapeDtypeStruct((m, heads * HEAD_W), out_dtype),
            jax.ShapeDtypeStruct((bsz, heads, HEAD_W, HEAD_W), F32),
        ],
        compiler_params=_params("parallel", "parallel"),
        name="retention",
    )(*args)


def _retention_tables(seq, first_pos, heads):
    half = HEAD_W // 2
    inv = 1.0 / (ROPE_BASE ** jnp.linspace(0.0, 1.0, half, dtype=F32))
    pos = (first_pos + jnp.arange(seq, dtype=jnp.int32)).astype(F32)
    ang = pos[:, None] * inv[None, :]
    chunk = seq if seq <= RET_CHUNK else RET_CHUNK
    lg = jnp.log(1.0 - 2.0 ** (-5.0 - jnp.arange(heads, dtype=F32)))
    idx = jnp.arange(chunk, dtype=F32)
    rel = idx[:, None] - idx[None, :]
    causal = rel >= 0
    dmask = jnp.where(causal[None], jnp.exp(jnp.where(causal, rel, 0.0)[None] * lg[:, None, None]), 0.0)
    qdec = jnp.exp((idx + 1.0)[None, :] * lg[:, None])[:, :, None]
    kdec = jnp.exp((chunk - 1.0 - idx)[None, :] * lg[:, None])[:, :, None]
    cdec = jnp.exp(chunk * lg)[:, None, None]
    return jnp.cos(ang), jnp.sin(ang), dmask, qdec, kdec, cdec


def kernel(x_prompt, x_sample, cache_k, cache_v, state_ret, page_table, c_prompt, c_sample, w_ada, b_ada, norm1, norm2, w_in, lambda_q1, lambda_k1, lambda_q2, lambda_k2, da_subln, w_pa, w_pr, w_o, w_up, w_down, norm_f):
    bp, seq, d = x_prompt.shape
    bs, n_new, _ = x_sample.shape
    depth = w_in.shape[0]
    page, heads = cache_k.shape[2], cache_k.shape[3]
    r_heads = state_ret.shape[2]
    assert cache_k.shape[4] == cache_v.shape[4] == state_ret.shape[3] == state_ret.shape[4] == HEAD_W
    assert heads == r_heads
    past_len = page_table.shape[1] * page
    d_ff = w_up.shape[2]
    hw = heads * HEAD_W
    col_k, col_v, col_qr = hw, 2 * hw, 3 * hw
    col_ga = 7 * hw

    c_rows = -(-(bp + bs) // 8) * 8
    c_all = jnp.concatenate([c_prompt, c_sample, jnp.zeros((c_rows - bp - bs, d), F32)], axis=0)
    mod = _ada(c_all, w_ada, b_ada)
    mod_p = mod[:, :bp].reshape(depth, bp, N_MOD, 1, d).transpose(0, 2, 1, 3, 4)
    mod_s = mod[:, bp:bp + bs].reshape(depth, bs, N_MOD, d).transpose(0, 2, 1, 3)
    mod_s = jnp.repeat(mod_s, n_new, axis=2)[:, :, None]

    lams = [a.reshape(depth, 1, -1) for a in (lambda_q1, lambda_k1, lambda_q2, lambda_k2)]
    subln = da_subln.reshape(depth, 1, HEAD_W)
    tab_p = _retention_tables(seq, 0, heads)
    tab_s = _retention_tables(n_new, past_len, heads)

    mp, ms = bp * seq, bs * n_new
    rows_all = mp + ms
    wide_all = (_tile16(rows_all, ROW_BLOCK_MAX),) + TILES_PROMPT_WIDE[1:]

    def norm_both(xp, xs, gain, l, m_scale, m_shift):
        h = _norm_mod(xp, gain, mod_p, l, m_scale, m_shift, rows_all)
        return _norm_mod(xs, gain, mod_s, l, m_scale, m_shift, rows_all, dst=h)

    xp = x_prompt.reshape(mp, d)
    xs = x_sample.reshape(ms, d)
    sp, ksm, vsm, ssm = [], [], [], []
    kv_p = None
    for l in range(depth):
        lam_init = 0.8 - 0.6 * math.exp(-0.3 * l)

        h = norm_both(xp, xs, norm1, l, 1, 0)
        proj = _matmul(h, w_in, l, w_in.shape[2], F32, wide_all)

        o_a_p, kv_p = _attn_prompt(proj, lams, subln, l, depth, kv_p, lam_init, bp, seq, heads)
        o_r_p, s_p = _retention(proj, 0, None, l, tab_p, bp, seq, heads, col_qr, 1, BF16, BF16)
        sp.append(s_p)

        k_s = proj[mp:, col_k:col_k + hw].reshape(bs, n_new, heads, HEAD_W)
        v_s = proj[mp:, col_v:col_v + hw].reshape(bs, n_new, heads, HEAD_W)
        o_a_s = _attn_decode(proj, mp, k_s, v_s, cache_k, cache_v, page_table, lams, subln, l, lam_init,
                             bs, n_new, heads)
        o_r_s, s_s = _retention(proj, mp, state_ret, l, tab_s, bs, n_new, heads, col_qr, bs, F32, F32)
        ksm.append(k_s)
        vsm.append(v_s)
        ssm.append(s_s)

        mixed_p = _mix(o_a_p, o_r_p, w_pa, w_pr, proj, 0, l, col_ga, col_ga + d, d, TILES_PROMPT_SQUARE)
        mixed_s = _mix(o_a_s, o_r_s.astype(BF16), w_pa, w_pr, proj, mp, l, col_ga, col_ga + d, d, TILES_DECODE)
        xp = _matmul(mixed_p, w_o, l, d, F32, TILES_PROMPT_SQUARE, "resid", xp, mod_p, 2)
        xs = _matmul(mixed_s, w_o, l, d, F32, TILES_DECODE, "resid", xs, mod_s, 2)

        h2 = norm_both(xp, xs, norm2, l, 4, 3)
        u = _matmul(h2, w_up, l, d_ff, BF16, wide_all, "relu2")
        xp = _matmul(u, w_down, l, d, F32, TILES_PROMPT_DEEP, "resid", xp, mod_p, 5, rows=(0, mp))
        xs = _matmul(u, w_down, l, d, F32, TILES_DECODE, "resid", xs, mod_s, 5, rows=(mp, ms))

    y_prompt = _final_norm(xp, norm_f).reshape(bp, seq, d)
    y_sample = _final_norm(xs, norm_f).reshape(bs, n_new, d)
    new_k_p, new_v_p = (a.reshape(depth, bp, seq, heads, HEAD_W) for a in kv_p)
    return (y_prompt, y_sample, new_k_p, new_v_p, jnp.stack(sp),
            jnp.stack(ksm), jnp.stack(vsm), jnp.stack(ssm))
```
